```python
import jax, jax.numpy as jnp
from jax import lax
import numpy as np

D_MODEL = 1024
BATCH = 2
SEQ = 16384
DEPTH = 1

D_MIX = D_MODEL
D_HG = D_MIX // 2
HG_HEAD_DIM = 128
HG_HEADS = D_HG // HG_HEAD_DIM
D_RG = D_MIX - D_HG
RG_BLOCKS = 8
RG_BLOCK_DIM = D_RG // RG_BLOCKS
RG_CONV_W = 4
RG_CONV_LEFT = RG_CONV_W // 2
RG_C = 8.0
D_FF = 2816
CHUNK = 64
EPS = 1e-6
N_MOD = 9
IN_COLS = 5 * D_HG + 2 * D_RG

kernel_name = "hymba_hgrn2_rglru_macaron_adaln"


def rms_norm(x, gain):
    xf = x.astype(jnp.float32)
    y = xf * lax.rsqrt(jnp.mean(xf * xf, axis=-1, keepdims=True) + EPS)
    return (y * gain.astype(jnp.float32)).astype(x.dtype)


def modulate(h, shift, scale):
    return h * (1 + scale[:, None, :]) + shift[:, None, :]


def swiglu(h, w1, w3, w2):
    return (jax.nn.silu(h @ w1) * (h @ w3)) @ w2


def gla_chunk_scan(q, k, v, log_g):
    B, T, H, dk = q.shape
    dv = v.shape[-1]
    n = T // CHUNK

    def to_chunks(a):
        return a.astype(jnp.float32).reshape(B, n, CHUNK, H, a.shape[-1]).transpose(1, 0, 3, 2, 4)

    qc, kc, vc, gc = to_chunks(q), to_chunks(k), to_chunks(v), to_chunks(log_g)
    mask = jnp.tril(jnp.ones((CHUNK, CHUNK), dtype=bool))[:, :, None]

    def step(S, inp):
        qb, kb, vb, gb = inp
        b = jnp.cumsum(gb, axis=-2)
        b_last = b[..., -1:, :]
        o_inter = jnp.einsum('bhck,bhkv->bhcv', qb * jnp.exp(b), S)
        diff = b[..., :, None, :] - b[..., None, :, :]
        decay = jnp.exp(jnp.where(mask, diff, -jnp.inf))
        scores = jnp.einsum('bhtk,bhsk,bhtsk->bhts', qb, kb, decay)
        o = o_inter + jnp.einsum('bhts,bhsv->bhtv', scores, vb)
        S = (jnp.exp(b_last[..., 0, :])[..., None] * S
             + jnp.einsum('bhck,bhcv->bhkv', kb * jnp.exp(b_last - b), vb))
        return S, o

    S0 = jnp.zeros((B, H, dk, dv), jnp.float32)
    _, o = lax.scan(step, S0, (qc, kc, vc, gc))
    return o.transpose(1, 0, 3, 2, 4).reshape(B, T, H, dv).astype(q.dtype)


def layer_lower_bound(lb_param, layer):
    s = jax.nn.softmax(lb_param.astype(jnp.float32), axis=0)
    return (jnp.cumsum(s, axis=0) - s[0])[layer + 1]


def hgrn2_mixer(q_raw, f_fwd_raw, f_bwd_raw, i_raw, g_raw, lb_fwd, lb_bwd, norm_gain):
    B, T, _ = q_raw.shape
    split = lambda a: a.reshape(B, T, HG_HEADS, HG_HEAD_DIM)
    q = split(jax.nn.silu(q_raw))
    v = split(i_raw)

    def direction(f_raw, lb, reverse):
        f = lb + (1 - lb) * jax.nn.sigmoid(f_raw.astype(jnp.float32))
        k = split((1 - f).astype(q_raw.dtype))
        lg = split(jnp.log(f))
        if reverse:
            return gla_chunk_scan(q[:, ::-1], k[:, ::-1], v[:, ::-1], lg[:, ::-1])[:, ::-1]
        return gla_chunk_scan(q, k, v, lg)

    o = direction(f_fwd_raw, lb_fwd, False) + direction(f_bwd_raw, lb_bwd, True)
    o = rms_norm(o, norm_gain).reshape(B, T, D_HG)
    return o * jax.nn.silu(g_raw)


def centred_dwconv(x, w, b):
    y = lax.conv_general_dilated(
        x, w[:, None, :].astype(x.dtype), window_strides=(1,),
        padding=[(RG_CONV_LEFT, RG_CONV_W - 1 - RG_CONV_LEFT)],
        dimension_numbers=('NWC', 'WIO', 'NWC'), feature_group_count=x.shape[-1])
    return y + b


def rglru_direction(xc, w_a, b_a, w_x, b_x, lam, reverse):
    B, T, _ = xc.shape
    xf = xc.astype(jnp.float32)
    xb = xf.reshape(B, T, RG_BLOCKS, RG_BLOCK_DIM)
    r = jax.nn.sigmoid(jnp.einsum('btnd,nde->btne', xb, w_a.astype(jnp.float32)).reshape(B, T, D_RG)
                       + b_a.astype(jnp.float32))
    i = jax.nn.sigmoid(jnp.einsum('btnd,nde->btne', xb, w_x.astype(jnp.float32)).reshape(B, T, D_RG)
                       + b_x.astype(jnp.float32))
    log_a = -RG_C * r * jax.nn.softplus(-lam.astype(jnp.float32))
    a = jnp.exp(log_a)
    u = jnp.sqrt(-jnp.expm1(2 * log_a)) * (i * xf)

    def combine(e1, e2):
        a1, b1 = e1
        a2, b2 = e2
        return a1 * a2, a2 * b1 + b2

    _, h = lax.associative_scan(combine, (a, u), axis=1, reverse=reverse)
    return h


def rglru_mixer(rx, rgate, conv_w, conv_b,
                wa_f, ba_f, wx_f, bx_f, lam_f, wa_b, ba_b, wx_b, bx_b, lam_b):
    xc = centred_dwconv(rx, conv_w, conv_b)
    h = (rglru_direction(xc, wa_f, ba_f, wx_f, bx_f, lam_f, False)
         + rglru_direction(xc, wa_b, ba_b, wx_b, bx_b, lam_b, True))
    return h.astype(rx.dtype) * jax.nn.gelu(rgate)


def setup_inputs(seed: int = 0) -> dict:
    key = jax.random.key(seed)
    ks = jax.random.split(key, 40)
    f32 = jnp.float32
    nrm = lambda k, shape, fan_in: jax.random.normal(k, shape, f32) * (fan_in ** -0.5)
    gain = lambda k, shape: 1.0 + 0.05 * jax.random.normal(k, shape, f32)
    small = lambda k, shape: 0.01 * jax.random.normal(k, shape, f32)

    def lam_init(k):
        a_c = jax.random.uniform(k, (DEPTH, D_RG), f32, 0.9, 0.999)
        a0 = a_c ** (1.0 / RG_C)
        return jnp.log(a0) - jnp.log1p(-a0)

    return {
        "x": jax.random.normal(ks[0], (BATCH, SEQ, D_MODEL), f32),
        "c": jax.random.normal(ks[1], (BATCH, D_MODEL), f32),
        "w_ada": nrm(ks[2], (DEPTH, D_MODEL, N_MOD * D_MODEL), D_MODEL),
        "b_ada": small(ks[3], (DEPTH, N_MOD * D_MODEL)),
        "ffn1_norm": gain(ks[4], (DEPTH, D_MODEL)),
        "ffn1_w1": nrm(ks[5], (DEPTH, D_MODEL, D_FF), D_MODEL),
        "ffn1_w3": nrm(ks[6], (DEPTH, D_MODEL, D_FF), D_MODEL),
        "ffn1_w2": nrm(ks[7], (DEPTH, D_FF, D_MODEL), D_FF),
        "mix_norm": gain(ks[8], (DEPTH, D_MODEL)),
        "w_in": nrm(ks[9], (DEPTH, D_MODEL, IN_COLS), D_MODEL),
        "hg_lb_fwd": jax.random.normal(ks[10], (DEPTH + 1, D_HG), f32),
        "hg_lb_bwd": jax.random.normal(ks[11], (DEPTH + 1, D_HG), f32),
        "hg_out_norm": gain(ks[12], (DEPTH, HG_HEAD_DIM)),
        "rg_conv_w": nrm(ks[13], (DEPTH, RG_CONV_W, D_RG), RG_CONV_W),
        "rg_conv_b": small(ks[14], (DEPTH, D_RG)),
        "rg_wa_fwd": nrm(ks[15], (DEPTH, RG_BLOCKS, RG_BLOCK_DIM, RG_BLOCK_DIM), RG_BLOCK_DIM),
        "rg_ba_fwd": small(ks[16], (DEPTH, D_RG)),
        "rg_wx_fwd": nrm(ks[17], (DEPTH, RG_BLOCKS, RG_BLOCK_DIM, RG_BLOCK_DIM), RG_BLOCK_DIM),
        "rg_bx_fwd": small(ks[18], (DEPTH, D_RG)),
        "rg_lam_fwd": lam_init(ks[19]),
        "rg_wa_bwd": nrm(ks[20], (DEPTH, RG_BLOCKS, RG_BLOCK_DIM, RG_BLOCK_DIM), RG_BLOCK_DIM),
        "rg_ba_bwd": small(ks[21], (DEPTH, D_RG)),
        "rg_wx_bwd": nrm(ks[22], (DEPTH, RG_BLOCKS, RG_BLOCK_DIM, RG_BLOCK_DIM), RG_BLOCK_DIM),
        "rg_bx_bwd": small(ks[23], (DEPTH, D_RG)),
        "rg_lam_bwd": lam_init(ks[24]),
        "w_out": nrm(ks[25], (DEPTH, D_MIX, D_MODEL), D_MIX),
        "ffn2_norm": gain(ks[26], (DEPTH, D_MODEL)),
        "ffn2_w1": nrm(ks[27], (DEPTH, D_MODEL, D_FF), D_MODEL),
        "ffn2_w3": nrm(ks[28], (DEPTH, D_MODEL, D_FF), D_MODEL),
        "ffn2_w2": nrm(ks[29], (DEPTH, D_FF, D_MODEL), D_FF),
        "final_norm": gain(ks[30], (D_MODEL,)),
    }


def reference(x, c, w_ada, b_ada, ffn1_norm, ffn1_w1, ffn1_w3, ffn1_w2, mix_norm, w_in,
              hg_lb_fwd, hg_lb_bwd, hg_out_norm, rg_conv_w, rg_conv_b,
              rg_wa_fwd, rg_ba_fwd, rg_wx_fwd, rg_bx_fwd, rg_lam_fwd,
              rg_wa_bwd, rg_ba_bwd, rg_wx_bwd, rg_bx_bwd, rg_lam_bwd,
              w_out, ffn2_norm, ffn2_w1, ffn2_w3, ffn2_w2, final_norm):
    split_idx = [D_HG, 2 * D_HG, 3 * D_HG, 4 * D_HG, 5 * D_HG, 5 * D_HG + D_RG]
    for l in range(DEPTH):
        mod = jax.nn.silu(c) @ w_ada[l] + b_ada[l]
        sh1, sc1, g1, sh2, sc2, g2, sh3, sc3, g3 = jnp.split(mod, N_MOD, axis=-1)

        h = modulate(rms_norm(x, ffn1_norm[l]), sh1, sc1)
        x = x + 0.5 * g1[:, None, :] * swiglu(h, ffn1_w1[l], ffn1_w3[l], ffn1_w2[l])

        h = modulate(rms_norm(x, mix_norm[l]), sh2, sc2)
        z = h @ w_in[l]
        q_raw, f_fwd_raw, f_bwd_raw, i_raw, g_raw, rx, rgate = jnp.split(z, split_idx, axis=-1)
        o_hg = hgrn2_mixer(q_raw, f_fwd_raw, f_bwd_raw, i_raw, g_raw,
                           layer_lower_bound(hg_lb_fwd, l), layer_lower_bound(hg_lb_bwd, l),
                           hg_out_norm[l])
        o_rg = rglru_mixer(rx, rgate, rg_conv_w[l], rg_conv_b[l],
                           rg_wa_fwd[l], rg_ba_fwd[l], rg_wx_fwd[l], rg_bx_fwd[l], rg_lam_fwd[l],
                           rg_wa_bwd[l], rg_ba_bwd[l], rg_wx_bwd[l], rg_bx_bwd[l], rg_lam_bwd[l])
        y = jnp.concatenate([o_hg, o_rg], axis=-1) @ w_out[l]
        x = x + g2[:, None, :] * y

        h = modulate(rms_norm(x, ffn2_norm[l]), sh3, sc3)
        x = x + 0.5 * g3[:, None, :] * swiglu(h, ffn2_w1[l], ffn2_w3[l], ffn2_w2[l])
    return rms_norm(x, final_norm)
```

```python
import functools

import jax
import jax.numpy as jnp
from jax import lax
from jax.experimental import pallas as pl
from jax.experimental.pallas import tpu as pltpu

F32 = jnp.float32
BF16 = jnp.bfloat16

EPS = 1e-6
N_MOD = 9
HG_HEAD_DIM = 128
HG_HEADS = 4
D_HG = HG_HEADS * HG_HEAD_DIM
RG_BLOCKS = 8
D_RG = 512
RG_CONV_W = 4
RG_CONV_LEFT = RG_CONV_W // 2
RG_C = 8.0
IN_COLS = 5 * D_HG + 2 * D_RG
COL_Q, COL_F_FWD, COL_F_BWD, COL_I, COL_G, COL_RX, COL_RGATE = range(7)

SUBLANES = 8
HALO = SUBLANES
HG_CHUNK = 64
VMEM_LIMIT = 56 * 1024 * 1024

ROW_TILE = 256
SCAN_TILE = 256


def _rms(x, gain):
    return x * lax.rsqrt(jnp.mean(x * x, axis=-1, keepdims=True) + EPS) * gain


def _norm_mod(x, gain, shift, scale):
    return _rms(x, gain) * (1.0 + scale) + shift


def _silu(x):
    return x * jax.nn.sigmoid(x)


def _dot(a, b):
    return jnp.dot(a, b, preferred_element_type=F32)


def _dot_nt(a, b):
    return lax.dot_general(a, b, (((1,), (1,)), ((), ())), preferred_element_type=F32)


def _dot_tn(a, b):
    return lax.dot_general(a, b, (((0,), (0,)), ((), ())), preferred_element_type=F32)


def _mod_kernel(c_ref, w_ref, b_ref, o_ref):
    o_ref[...] = _dot(_silu(c_ref[...]), w_ref[...]) + b_ref[...]


def _modulation(c, w_ada, b_ada):
    bsz, d = c.shape
    n = w_ada.shape[1]
    bn = 1024
    c_pad = jnp.pad(c, ((0, SUBLANES - bsz), (0, 0)))
    out = pl.pallas_call(
        _mod_kernel,
        grid=(n // bn,),
        in_specs=[
            pl.BlockSpec((SUBLANES, d), lambda j: (0, 0)),
            pl.BlockSpec((d, bn), lambda j: (0, j)),
            pl.BlockSpec((1, bn), lambda j: (0, j)),
        ],
        out_specs=pl.BlockSpec((SUBLANES, bn), lambda j: (0, j)),
        out_shape=jax.ShapeDtypeStruct((SUBLANES, n), F32),
        name="modulation",
    )(c_pad, w_ada, b_ada.reshape(1, n))
    return out[:bsz].reshape(bsz, N_MOD, d)


def _ffn_kernel(x_ref, mod_ref, gain_ref, w1_ref, w3_ref, w2_ref, fgain_ref, o_ref, *, mod_base, final_norm):
    x = x_ref[...]
    shift = mod_ref[mod_base:mod_base + 1, :]
    scale = mod_ref[mod_base + 1:mod_base + 2, :]
    gate = mod_ref[mod_base + 2:mod_base + 3, :]
    h = _norm_mod(x, gain_ref[...], shift, scale).astype(BF16)
    a = _dot(h, w1_ref[...])
    b = _dot(h, w3_ref[...])
    g = (_silu(a) * b).astype(BF16)
    out = x + (0.5 * gate) * _dot(g, w2_ref[...])
    if final_norm:
        out = _rms(out, fgain_ref[...])
    o_ref[...] = out


def _resident(shape):
    return pl.BlockSpec(shape, lambda *_: (0,) * len(shape), pipeline_mode=pl.Buffered(1))


def _ffn(x2d, mod, gain, w1, w3, w2, final_gain, *, mod_base, final_norm, tiles_per_batch):
    n, d = x2d.shape
    f = w1.shape[1]
    tm = ROW_TILE
    kern = functools.partial(_ffn_kernel, mod_base=mod_base, final_norm=final_norm)
    return pl.pallas_call(
        kern,
        grid=(n // tm,),
        in_specs=[
            pl.BlockSpec((tm, d), lambda i: (i, 0)),
            pl.BlockSpec((None, N_MOD, d), lambda i: (i // tiles_per_batch, 0, 0)),
            _resident((1, d)),
            _resident((d, f)),
            _resident((d, f)),
            _resident((f, d)),
            _resident((1, d)),
        ],
        out_specs=pl.BlockSpec((tm, d), lambda i: (i, 0)),
        out_shape=jax.ShapeDtypeStruct((n, d), F32),
        compiler_params=pltpu.CompilerParams(
            dimension_semantics=("arbitrary",), vmem_limit_bytes=VMEM_LIMIT),
        name="ffn_final" if final_norm else "ffn",
    )(x2d, mod, gain, w1, w3, w2, final_gain)


def _inproj_kernel(x_ref, mod_ref, gain_ref, w_ref, z_ref):
    shift = mod_ref[3:4, :]
    scale = mod_ref[4:5, :]
    h = _norm_mod(x_ref[...], gain_ref[...], shift, scale).astype(BF16)
    z_ref[...] = _dot(h, w_ref[...])


def _inproj(x2d, mod, gain, w_in, *, tiles_per_batch):
    n, d = x2d.shape
    cols = w_in.shape[1]
    tm = ROW_TILE
    return pl.pallas_call(
        _inproj_kernel,
        grid=(n // tm,),
        in_specs=[
            pl.BlockSpec((tm, d), lambda i: (i, 0)),
            pl.BlockSpec((None, N_MOD, d), lambda i: (i // tiles_per_batch, 0, 0)),
            _resident((1, d)),
            _resident((d, cols)),
        ],
        out_specs=pl.BlockSpec((tm, cols), lambda i: (i, 0)),
        out_shape=jax.ShapeDtypeStruct((n, cols), F32),
        compiler_params=pltpu.CompilerParams(
            dimension_semantics=("arbitrary",), vmem_limit_bytes=VMEM_LIMIT),
        name="inproj",
    )(x2d, mod, gain, w_in)


def _lower_bound(lb_ref):
    p0 = lb_ref[0:1, :]
    p1 = lb_ref[1:2, :]
    m = jnp.maximum(p0, p1)
    e0 = jnp.exp(p0 - m)
    e1 = jnp.exp(p1 - m)
    return e1 / (e0 + e1)


def _split3(x):
    hi = x.astype(BF16)
    r = x - hi.astype(F32)
    mid = r.astype(BF16)
    lo = (r - mid.astype(F32)).astype(BF16)
    return hi, mid, lo


def _hgrn_chunk(q_ref, f_ref, v_ref, o_ref, s_ref, r0, lb, tri, *, reverse):
    c = HG_CHUNK
    rows = pl.ds(r0, c)
    qr = q_ref[rows, :]
    v = v_ref[rows, :]
    q = _silu(qr)
    f = lb + (1.0 - lb) * jax.nn.sigmoid(f_ref[rows, :])
    k = 1.0 - f
    hi, mid, lo = _split3(jnp.log(f))
    b = _dot(tri, hi) + _dot(tri, mid) + _dot(tri, lo)
    if reverse:
        b_end, b_ref_row = b[0:1, :], b[c // 2:c // 2 + 1, :]
    else:
        b_end, b_ref_row = b[c - 1:c, :], b[c // 2 - 1:c // 2, :]
    q_inter = (q * jnp.exp(b)).astype(BF16)
    k_state = (k * jnp.exp(b_end - b)).astype(BF16)
    q_intra = (q * jnp.exp(b - b_ref_row)).astype(BF16)
    k_intra = (k * jnp.exp(b_ref_row - b)).astype(BF16)
    decay_end = jnp.exp(b_end)
    v16 = v.astype(BF16)
    causal = tri > 0
    for h in range(HG_HEADS):
        sl = slice(h * HG_HEAD_DIM, (h + 1) * HG_HEAD_DIM)
        st = s_ref[h]
        o_inter = _dot_nt(q_inter[:, sl], st.astype(BF16))
        scores = jnp.where(causal, _dot_nt(q_intra[:, sl], k_intra[:, sl]), 0.0)
        o_ref[rows, sl] = o_inter + _dot(scores.astype(BF16), v16[:, sl])
        s_ref[h] = st * decay_end[:, sl] + _dot_tn(v16[:, sl], k_state[:, sl])


def _hgrn_kernel(qf_ref, ff_ref, vf_ref, qb_ref, fb_ref, vb_ref, lbf_ref, lbb_ref, of_ref, ob_ref, s_ref):
    tm = of_ref.shape[0]
    n_chunks = tm // HG_CHUNK

    @pl.when(pl.program_id(1) == 0)
    def _():
        s_ref[...] = jnp.zeros_like(s_ref)

    lb_f = _lower_bound(lbf_ref)
    lb_b = _lower_bound(lbb_ref)
    row = lax.broadcasted_iota(jnp.int32, (HG_CHUNK, HG_CHUNK), 0)
    col = lax.broadcasted_iota(jnp.int32, (HG_CHUNK, HG_CHUNK), 1)
    tri_f = jnp.where(col <= row, 1.0, 0.0).astype(BF16)
    tri_b = jnp.where(col >= row, 1.0, 0.0).astype(BF16)

    def body(ci, carry):
        r_f = pl.multiple_of(ci * HG_CHUNK, HG_CHUNK)
        r_b = pl.multiple_of((n_chunks - 1 - ci) * HG_CHUNK, HG_CHUNK)
        _hgrn_chunk(qf_ref, ff_ref, vf_ref, of_ref, s_ref.at[0], r_f, lb_f, tri_f, reverse=False)
        _hgrn_chunk(qb_ref, fb_ref, vb_ref, ob_ref, s_ref.at[1], r_b, lb_b, tri_b, reverse=True)
        return carry

    lax.fori_loop(0, n_chunks, body, 0)


def _hgrn_scan(z3d, lb_fwd, lb_bwd):
    bsz, t, _ = z3d.shape
    tm = SCAN_TILE
    n_t = t // tm

    def fwd(col):
        return pl.BlockSpec((None, tm, D_HG), lambda b, j: (b, j, col))

    def bwd(col):
        return pl.BlockSpec((None, tm, D_HG), lambda b, j: (b, n_t - 1 - j, col))

    lb_spec = pl.BlockSpec(lb_fwd.shape, lambda b, j: (0, 0))
    return pl.pallas_call(
        _hgrn_kernel,
        grid=(bsz, n_t),
        in_specs=[fwd(COL_Q), fwd(COL_F_FWD), fwd(COL_I), bwd(COL_Q), bwd(COL_F_BWD), bwd(COL_I),
                  lb_spec, lb_spec],
        out_specs=[pl.BlockSpec((None, tm, D_HG), lambda b, j: (b, j, 0)),
                   pl.BlockSpec((None, tm, D_HG), lambda b, j: (b, n_t - 1 - j, 0))],
        out_shape=[jax.ShapeDtypeStruct((bsz, t, D_HG), F32)] * 2,
        scratch_shapes=[pltpu.VMEM((2, HG_HEADS, HG_HEAD_DIM, HG_HEAD_DIM), F32)],
        compiler_params=pltpu.CompilerParams(
            dimension_semantics=("arbitrary", "arbitrary"), vmem_limit_bytes=VMEM_LIMIT),
        name="hgrn_scan",
    )(z3d, z3d, z3d, z3d, z3d, z3d, lb_fwd, lb_bwd)


def _softplus(x):
    return jnp.maximum(x, 0.0) + jnp.log1p(jnp.exp(-jnp.abs(x)))


def _linear_scan(a, u, *, reverse):
    n = a.shape[0]
    row = lax.broadcasted_iota(jnp.int32, a.shape, 0)
    s = 1
    while s < n:
        if reverse:
            keep = row < n - s
            a_s = jnp.where(keep, pltpu.roll(a, n - s, 0), 1.0)
            u_s = jnp.where(keep, pltpu.roll(u, n - s, 0), 0.0)
        else:
            keep = row >= s
            a_s = jnp.where(keep, pltpu.roll(a, s, 0), 1.0)
            u_s = jnp.where(keep, pltpu.roll(u, s, 0), 0.0)
        u = u + a * u_s
        a = a * a_s
        s *= 2
    return a, u


def _rglru_direction(x_ref, prev_ref, next_ref, xe_ref, cw_ref, cb_ref, wa_ref, ba_ref, wx_ref, bx_ref,
                     lam_ref, h_ref, carry_ref, first_tile, last_tile, *, reverse):
    tm = x_ref.shape[0]
    xe_ref[0:HALO, :] = jnp.where(first_tile, 0.0, prev_ref[...])
    xe_ref[HALO:HALO + tm, :] = x_ref[...]
    xe_ref[HALO + tm:HALO + tm + HALO, :] = jnp.where(last_tile, 0.0, next_ref[...])
    xc = cb_ref[...] + xe_ref[HALO - RG_CONV_LEFT:HALO - RG_CONV_LEFT + tm, :] * cw_ref[0:1, :]
    for w in range(1, RG_CONV_W):
        xc = xc + xe_ref[HALO - RG_CONV_LEFT + w:HALO - RG_CONV_LEFT + w + tm, :] * cw_ref[w:w + 1, :]
    xc16 = xc.astype(BF16)
    r = jax.nn.sigmoid(_dot(xc16, wa_ref[...]) + ba_ref[...])
    i = jax.nn.sigmoid(_dot(xc16, wx_ref[...]) + bx_ref[...])
    log_a = (-RG_C) * r * _softplus(-lam_ref[...])
    a = jnp.exp(log_a)
    u = jnp.sqrt(-jnp.tanh(log_a) * (a * a + 1.0)) * (i * xc)
    a_cum, h0 = _linear_scan(a, u, reverse=reverse)
    h = h0 + a_cum * carry_ref[...]
    h_ref[...] = h
    carry_ref[...] = h[0:1, :] if reverse else h[tm - 1:tm, :]


def _rglru_kernel(xf_ref, pf_ref, nf_ref, xb_ref, pb_ref, nb_ref, cw_ref, cb_ref,
                  waf_ref, baf_ref, wxf_ref, bxf_ref, lamf_ref,
                  wab_ref, bab_ref, wxb_ref, bxb_ref, lamb_ref,
                  hf_ref, hb_ref, xe_ref, carry_ref):
    j = pl.program_id(1)
    n_t = pl.num_programs(1)

    @pl.when(j == 0)
    def _():
        carry_ref[...] = jnp.zeros_like(carry_ref)

    _rglru_direction(xf_ref, pf_ref, nf_ref, xe_ref.at[0], cw_ref, cb_ref, waf_ref, baf_ref, wxf_ref, bxf_ref,
                     lamf_ref, hf_ref, carry_ref.at[0], j == 0, j == n_t - 1, reverse=False)
    _rglru_direction(xb_ref, pb_ref, nb_ref, xe_ref.at[1], cw_ref, cb_ref, wab_ref, bab_ref, wxb_ref, bxb_ref,
                     lamb_ref, hb_ref, carry_ref.at[1], j == n_t - 1, j == 0, reverse=True)


def _rglru_scan(z3d, conv_w, conv_b, gates_fwd, gates_bwd):
    bsz, t, _ = z3d.shape
    tm = SCAN_TILE
    n_t = t // tm
    halo_per_tile = tm // HALO
    n_halo = t // HALO

    def tile_specs(tile_of):
        main = pl.BlockSpec((None, tm, D_RG), lambda b, j: (b, tile_of(j), COL_RX))
        prev = pl.BlockSpec(
            (None, HALO, D_RG), lambda b, j: (b, jnp.maximum(tile_of(j) * halo_per_tile - 1, 0), COL_RX))
        nxt = pl.BlockSpec(
            (None, HALO, D_RG),
            lambda b, j: (b, jnp.minimum((tile_of(j) + 1) * halo_per_tile, n_halo - 1), COL_RX))
        return [main, prev, nxt]

    def whole(a):
        return pl.BlockSpec(a.shape, lambda b, j: (0,) * a.ndim)

    params = [conv_w, conv_b, *gates_fwd, *gates_bwd]
    return pl.pallas_call(
        _rglru_kernel,
        grid=(bsz, n_t),
        in_specs=tile_specs(lambda j: j) + tile_specs(lambda j: n_t - 1 - j) + [whole(p) for p in params],
        out_specs=[pl.BlockSpec((None, tm, D_RG), lambda b, j: (b, j, 0)),
                   pl.BlockSpec((None, tm, D_RG), lambda b, j: (b, n_t - 1 - j, 0))],
        out_shape=[jax.ShapeDtypeStruct((bsz, t, D_RG), F32)] * 2,
        scratch_shapes=[pltpu.VMEM((2, tm + 2 * HALO, D_RG), F32), pltpu.VMEM((2, 1, D_RG), F32)],
        compiler_params=pltpu.CompilerParams(
            dimension_semantics=("arbitrary", "arbitrary"), vmem_limit_bytes=VMEM_LIMIT),
        name="rglru_scan",
    )(z3d, z3d, z3d, z3d, z3d, z3d, *params)


def _gelu_tanh(x):
    return 0.5 * x * (1.0 + jnp.tanh(0.7978845608028654 * (x + 0.044715 * (x * x * x))))


def _outproj_kernel(x_ref, mod_ref, of_ref, ob_ref, g_ref, hf_ref, hb_ref, rgate_ref, hgain_ref, w_ref, o_ref):
    gate = mod_ref[5:6, :]
    o = of_ref[...] + ob_ref[...]
    g = _silu(g_ref[...])
    heads = []
    for h in range(HG_HEADS):
        sl = slice(h * HG_HEAD_DIM, (h + 1) * HG_HEAD_DIM)
        heads.append((_rms(o[:, sl], hgain_ref[...]) * g[:, sl]).astype(BF16))
    o_rg = ((hf_ref[...] + hb_ref[...]) * _gelu_tanh(rgate_ref[...])).astype(BF16)
    y = _dot(o_rg, w_ref[D_HG:, :])
    for h in range(HG_HEADS):
        y = y + _dot(heads[h], w_ref[h * HG_HEAD_DIM:(h + 1) * HG_HEAD_DIM, :])
    o_ref[...] = x_ref[...] + gate * y


def _outproj(x2d, mod, o_f, o_b, z2d, h_f, h_b, hg_gain, w_out, *, tiles_per_batch):
    n, d = x2d.shape
    tm = ROW_TILE

    def rows(width, col=0):
        return pl.BlockSpec((tm, width), lambda i: (i, col))

    return pl.pallas_call(
        _outproj_kernel,
        grid=(n // tm,),
        in_specs=[
            rows(d),
            pl.BlockSpec((None, N_MOD, d), lambda i: (i // tiles_per_batch, 0, 0)),
            rows(D_HG), rows(D_HG), rows(D_HG, COL_G), rows(D_RG), rows(D_RG), rows(D_RG, COL_RGATE),
            _resident((1, HG_HEAD_DIM)),
            _resident(w_out.shape),
        ],
        out_specs=rows(d),
        out_shape=jax.ShapeDtypeStruct((n, d), F32),
        compiler_params=pltpu.CompilerParams(
            dimension_semantics=("arbitrary",), vmem_limit_bytes=VMEM_LIMIT),
        name="outproj",
    )(x2d, mod, o_f, o_b, z2d, h_f, h_b, z2d, hg_gain, w_out)


def _block_diag(w):
    nb, d, e = w.shape
    eye = jnp.eye(nb, dtype=w.dtype)
    return (eye[:, None, :, None] * w[:, :, None, :]).reshape(nb * d, nb * e)


def kernel(x, c, w_ada, b_ada, ffn1_norm, ffn1_w1, ffn1_w3, ffn1_w2, mix_norm, w_in, hg_lb_fwd, hg_lb_bwd, hg_out_norm, rg_conv_w, rg_conv_b, rg_wa_fwd, rg_ba_fwd, rg_wx_fwd, rg_bx_fwd, rg_lam_fwd, rg_wa_bwd, rg_ba_bwd, rg_wx_bwd, rg_bx_bwd, rg_lam_bwd, w_out, ffn2_norm, ffn2_w1, ffn2_w3, ffn2_w2, final_norm):
    bsz, t, d = x.shape
    depth = w_ada.shape[0]
    assert depth == 1 and hg_lb_fwd.shape[0] == 2, "kernels are written for a single layer"
    assert t % ROW_TILE == 0 and t % SCAN_TILE == 0 and SCAN_TILE % HG_CHUNK == 0
    tiles_per_batch = t // ROW_TILE
    row = lambda v: v.reshape(1, -1)
    l = 0

    mod = _modulation(c, w_ada[l], b_ada[l])
    x2d = x.reshape(bsz * t, d)
    fgain = row(final_norm)

    x2d = _ffn(x2d, mod, row(ffn1_norm[l]), ffn1_w1[l].astype(BF16), ffn1_w3[l].astype(BF16),
               ffn1_w2[l].astype(BF16), fgain, mod_base=0, final_norm=False, tiles_per_batch=tiles_per_batch)

    z2d = _inproj(x2d, mod, row(mix_norm[l]), w_in[l].astype(BF16), tiles_per_batch=tiles_per_batch)
    z3d = z2d.reshape(bsz, t, IN_COLS)
    o_f, o_b = _hgrn_scan(z3d, hg_lb_fwd, hg_lb_bwd)

    def gates(wa, ba, wx, bx, lam):
        return (_block_diag(wa[l]).astype(BF16), row(ba[l]), _block_diag(wx[l]).astype(BF16), row(bx[l]),
                row(lam[l]))

    h_f, h_b = _rglru_scan(z3d, rg_conv_w[l], row(rg_conv_b[l]),
                           gates(rg_wa_fwd, rg_ba_fwd, rg_wx_fwd, rg_bx_fwd, rg_lam_fwd),
                           gates(rg_wa_bwd, rg_ba_bwd, rg_wx_bwd, rg_bx_bwd, rg_lam_bwd))

    flat = lambda a: a.reshape(bsz * t, a.shape[-1])
    x2d = _outproj(x2d, mod, flat(o_f), flat(o_b), z2d, flat(h_f), flat(h_b), row(hg_out_norm[l]),
                   w_out[l].astype(BF16), tiles_per_batch=tiles_per_batch)

    out = _ffn(x2d, mod, row(ffn2_norm[l]), ffn2_w1[l].astype(BF16), ffn2_w3[l].astype(BF16),
               ffn2_w2[l].astype(BF16), fgain, mod_base=6, final_norm=True, tiles_per_batch=tiles_per_batch)
    return out.reshape(bsz, t, d)
```

```python
import functools
import math
from typing import NamedTuple

import jax
import jax.numpy as jnp
from jax import lax
from jax.experimental import pallas as pl
from jax.experimental.pallas import tpu as pltpu

F32 = jnp.float32
BF16 = jnp.bfloat16

EPS = 1e-6
N_MOD = 9
HG_HEAD_DIM = 128
HG_HEADS = 4
D_HG = HG_HEADS * HG_HEAD_DIM
RG_BLOCKS = 8
D_RG = 512
RG_CONV_W = 4
RG_CONV_LEFT = RG_CONV_W // 2
RG_C = 8.0
IN_COLS = 5 * D_HG + 2 * D_RG
COL_Q, COL_F_FWD, COL_F_BWD, COL_I, COL_G, COL_RX, COL_RGATE = range(7)

SUBLANES = 8
LANES = 128
HALO = SUBLANES
HG_CHUNK = 64
LOG2_E = math.log2(math.e)
HG_SAFE_EXPONENT_LOG2 = 115.0
VMEM_LIMIT = 56 * 1024 * 1024

ROW_TILE = 256
SCAN_TILE = 256


def _rms(x, gain):
    return x * lax.rsqrt(jnp.mean(x * x, axis=-1, keepdims=True) + EPS) * gain


def _norm_mod(x, gain, shift, scale):
    return _rms(x, gain) * (1.0 + scale) + shift


def _silu(x):
    return x * jax.nn.sigmoid(x)


def _dot(a, b):
    return jnp.dot(a, b, preferred_element_type=F32)


def _dot_nt(a, b):
    return lax.dot_general(a, b, (((1,), (1,)), ((), ())), preferred_element_type=F32)


def _dot_tn(a, b):
    return lax.dot_general(a, b, (((0,), (0,)), ((), ())), preferred_element_type=F32)


def _mod_kernel(c_ref, w_ref, b_ref, o_ref):
    o_ref[...] = _dot(_silu(c_ref[...]), w_ref[...]) + b_ref[...]


def _modulation(c, w_ada, b_ada):
    bsz, d = c.shape
    n = w_ada.shape[1]
    bn = 1024
    c_pad = jnp.pad(c, ((0, SUBLANES - bsz), (0, 0)))
    out = pl.pallas_call(
        _mod_kernel,
        grid=(n // bn,),
        in_specs=[
            pl.BlockSpec((SUBLANES, d), lambda j: (0, 0)),
            pl.BlockSpec((d, bn), lambda j: (0, j)),
            pl.BlockSpec((1, bn), lambda j: (0, j)),
        ],
        out_specs=pl.BlockSpec((SUBLANES, bn), lambda j: (0, j)),
        out_shape=jax.ShapeDtypeStruct((SUBLANES, n), F32),
        name="modulation",
    )(c_pad, w_ada, b_ada.reshape(1, n))
    return out[:bsz].reshape(bsz, N_MOD, d)


def _ffn_kernel(x_ref, mod_ref, gain_ref, w1_ref, w3_ref, w2_ref, fgain_ref, o_ref, *, mod_base, final_norm):
    x = x_ref[...]
    shift = mod_ref[mod_base:mod_base + 1, :]
    scale = mod_ref[mod_base + 1:mod_base + 2, :]
    gate = mod_ref[mod_base + 2:mod_base + 3, :]
    h = _norm_mod(x, gain_ref[...], shift, scale).astype(BF16)
    a = _dot(h, w1_ref[...])
    b = _dot(h, w3_ref[...])
    g = (_silu(a) * b).astype(BF16)
    out = x + (0.5 * gate) * _dot(g, w2_ref[...])
    if final_norm:
        out = _rms(out, fgain_ref[...])
    o_ref[...] = out


def _resident(shape):
    return pl.BlockSpec(shape, lambda *_: (0,) * len(shape), pipeline_mode=pl.Buffered(1))


def _ffn(x2d, mod, gain, w1, w3, w2, final_gain, *, mod_base, final_norm, tiles_per_batch):
    n, d = x2d.shape
    f = w1.shape[1]
    tm = ROW_TILE
    kern = functools.partial(_ffn_kernel, mod_base=mod_base, final_norm=final_norm)
    return pl.pallas_call(
        kern,
        grid=(n // tm,),
        in_specs=[
            pl.BlockSpec((tm, d), lambda i: (i, 0)),
            pl.BlockSpec((None, N_MOD, d), lambda i: (i // tiles_per_batch, 0, 0)),
            _resident((1, d)),
            _resident((d, f)),
            _resident((d, f)),
            _resident((f, d)),
            _resident((1, d)),
        ],
        out_specs=pl.BlockSpec((tm, d), lambda i: (i, 0)),
        out_shape=jax.ShapeDtypeStruct((n, d), F32),
        compiler_params=pltpu.CompilerParams(
            dimension_semantics=("arbitrary",), vmem_limit_bytes=VMEM_LIMIT),
        name="ffn_final" if final_norm else "ffn",
    )(x2d, mod, gain, w1, w3, w2, final_gain)


def _inproj_kernel(x_ref, mod_ref, gain_ref, w_ref, z_ref):
    shift = mod_ref[3:4, :]
    scale = mod_ref[4:5, :]
    h = _norm_mod(x_ref[...], gain_ref[...], shift, scale).astype(BF16)
    z_ref[...] = _dot(h, w_ref[...])


def _inproj(x2d, mod, gain, w_in, *, tiles_per_batch):
    n, d = x2d.shape
    cols = w_in.shape[1]
    tm = ROW_TILE
    return pl.pallas_call(
        _inproj_kernel,
        grid=(n // tm,),
        in_specs=[
            pl.BlockSpec((tm, d), lambda i: (i, 0)),
            pl.BlockSpec((None, N_MOD, d), lambda i: (i // tiles_per_batch, 0, 0)),
            _resident((1, d)),
            _resident((d, cols)),
        ],
        out_specs=pl.BlockSpec((tm, cols), lambda i: (i, 0)),
        out_shape=jax.ShapeDtypeStruct((n, cols), F32),
        compiler_params=pltpu.CompilerParams(
            dimension_semantics=("arbitrary",), vmem_limit_bytes=VMEM_LIMIT),
        name="inproj",
    )(x2d, mod, gain, w_in)


def _lower_bound(lb_ref):
    p0 = lb_ref[0:1, :]
    p1 = lb_ref[1:2, :]
    m = jnp.maximum(p0, p1)
    e0 = jnp.exp(p0 - m)
    e1 = jnp.exp(p1 - m)
    return e1 / (e0 + e1)


def _split3(x):
    hi = x.astype(BF16)
    r = x - hi.astype(F32)
    mid = r.astype(BF16)
    lo = (r - mid.astype(F32)).astype(BF16)
    return hi, mid, lo


class _HgrnScratch(NamedTuple):
    q: object
    k: object
    lg3: object
    b: object
    v16: object
    qm: object
    km: object
    qi: object
    ks: object
    ds: object
    dec: object
    wide: object


def _chunk_rows(c):
    return slice(c * HG_CHUNK, (c + 1) * HG_CHUNK)


def _head_lanes(h):
    return slice(h * HG_HEAD_DIM, (h + 1) * HG_HEAD_DIM)


def _chunk_marks(reverse):
    c = HG_CHUNK
    return (c - 1, 0, c // 2) if reverse else (0, c - 1, c // 2 - 1)


def _hgrn_elementwise(q_ref, f_ref, v_ref, lb, sc):
    sc.q[...] = _silu(q_ref[...])
    f = lb + (1.0 - lb) * jax.nn.sigmoid(f_ref[...])
    sc.k[...] = 1.0 - f
    hi, mid, lo = _split3(jnp.log(f))
    sc.lg3[:, 0:D_HG] = hi
    sc.lg3[:, D_HG:2 * D_HG] = mid
    sc.lg3[:, 2 * D_HG:3 * D_HG] = lo
    sc.v16[...] = v_ref[...].astype(BF16)


def _hgrn_cumsum(sc, tri, n_chunks):
    for c in range(n_chunks):
        rows = _chunk_rows(c)
        p = _dot(tri, sc.lg3[rows, :])
        sc.b[rows, :] = p[:, 0:D_HG] + p[:, D_HG:2 * D_HG] + p[:, 2 * D_HG:3 * D_HG]


def _hgrn_factors(sc, n_chunks, *, reverse):
    first, last, ref = _chunk_marks(reverse)
    n_wide = None
    for c in range(n_chunks):
        rows = _chunk_rows(c)
        b2 = sc.b[rows, :] * LOG2_E
        r2 = b2[ref:ref + 1, :]
        end2 = b2[last:last + 1, :]
        spread = jnp.maximum(b2[first:first + 1, :] - r2, r2 - end2)
        wide = jnp.where(spread < HG_SAFE_EXPONENT_LOG2, 0.0, 1.0)
        sc.wide[c:c + 1, :] = wide
        n_wide = wide if n_wide is None else n_wide + wide
        narrow = jnp.broadcast_to(wide, b2.shape) == 0.0
        q, k = sc.q[rows, :], sc.k[rows, :]
        sc.qm[rows, :] = jnp.where(narrow, q * jnp.exp2(b2 - r2), 0.0).astype(BF16)
        sc.km[rows, :] = jnp.where(narrow, k * jnp.exp2(r2 - b2), 0.0).astype(BF16)
        sc.qi[rows, :] = (q * jnp.exp2(b2)).astype(BF16)
        sc.ks[rows, :] = (k * jnp.exp2(end2 - b2)).astype(BF16)
        sc.dec[c:c + 1, :] = jnp.exp2(end2)
    return n_wide


def _hgrn_state_increment(sc, c, h):
    rows, sl = _chunk_rows(c), _head_lanes(h)
    sc.ds[c, h] = _dot_tn(sc.v16[rows, sl], sc.ks[rows, sl])


def _hgrn_intra_factored(sc, o_ref, tri, n_chunks):
    causal = tri > 0
    pairs = [(c, h) for c in range(n_chunks) for h in range(HG_HEADS)]
    for c, h in pairs:
        _hgrn_state_increment(sc, c, h)
    scores = {}
    for c, h in pairs:
        rows, sl = _chunk_rows(c), _head_lanes(h)
        scores[c, h] = jnp.where(causal, _dot_nt(sc.qm[rows, sl], sc.km[rows, sl]), 0.0).astype(BF16)
    for c, h in pairs:
        rows, sl = _chunk_rows(c), _head_lanes(h)
        o_ref[rows, sl] = _dot(scores[c, h], sc.v16[rows, sl])


def _hgrn_intra_halving(sc, o_ref, n_chunks, *, reverse):
    cs = HG_CHUNK
    row = lax.broadcasted_iota(jnp.int32, (cs, cs), 0)
    col = lax.broadcasted_iota(jnp.int32, (cs, cs), 1)
    pos = lax.broadcasted_iota(jnp.int32, (cs, D_HG), 0)
    for c in range(n_chunks):
        rows = _chunk_rows(c)
        wide = jnp.broadcast_to(sc.wide[c:c + 1, :], (cs, D_HG)) > 0.0
        b, k = sc.b[rows, :], sc.k[rows, :]
        q = jnp.where(wide, sc.q[rows, :], 0.0)
        q16, k16 = q.astype(BF16), k.astype(BF16)
        scores = [jnp.where(row == col, _dot_nt(q16[:, _head_lanes(h)], k16[:, _head_lanes(h)]), 0.0)
                  for h in range(HG_HEADS)]
        b_hi, b_mid, b_lo = _split3(b)
        shift = 1
        while (1 << shift) <= cs:
            blk, half = 1 << shift, 1 << (shift - 1)
            boundary = ((row >> shift) << shift) + (half if reverse else half - 1)
            pick = jnp.where(col == boundary, 1.0, 0.0).astype(BF16)
            b_bnd = _dot(pick, b_hi) + _dot(pick, b_mid) + _dot(pick, b_lo)
            in_half = (pos & (blk - 1)) >= half
            attends = jnp.logical_not(in_half) if reverse else in_half
            qh = jnp.where(attends, q * jnp.exp(b - b_bnd), 0.0).astype(BF16)
            kh = jnp.where(attends, 0.0, k * jnp.exp(b_bnd - b)).astype(BF16)
            same_block = (row >> shift) == (col >> shift)
            for h in range(HG_HEADS):
                sl = _head_lanes(h)
                scores[h] = scores[h] + jnp.where(same_block, _dot_nt(qh[:, sl], kh[:, sl]), 0.0)
            shift += 1
        for h in range(HG_HEADS):
            sl = _head_lanes(h)
            o_ref[rows, sl] += _dot(scores[h].astype(BF16), sc.v16[rows, sl])


def _hgrn_carry_state(sc, o_ref, s_ref, n_chunks, *, reverse):
    order = range(n_chunks - 1, -1, -1) if reverse else range(n_chunks)
    for c in order:
        rows = _chunk_rows(c)
        for h in range(HG_HEADS):
            sl = _head_lanes(h)
            st = s_ref[h]
            o_ref[rows, sl] += _dot_nt(sc.qi[rows, sl], st.astype(BF16))
            s_ref[h] = st * sc.dec[c:c + 1, sl] + sc.ds[c, h]


def _hgrn_kernel(qf_ref, ff_ref, vf_ref, qb_ref, fb_ref, vb_ref, lbf_ref, lbb_ref, of_ref, ob_ref, s_ref, *scratch):
    tm = of_ref.shape[0]
    n_chunks = tm // HG_CHUNK
    sc_f = _HgrnScratch(*(r.at[0] for r in scratch))
    sc_b = _HgrnScratch(*(r.at[1] for r in scratch))

    @pl.when(pl.program_id(1) == 0)
    def _():
        s_ref[...] = jnp.zeros_like(s_ref)

    row = lax.broadcasted_iota(jnp.int32, (HG_CHUNK, HG_CHUNK), 0)
    col = lax.broadcasted_iota(jnp.int32, (HG_CHUNK, HG_CHUNK), 1)
    tri_f = jnp.where(col <= row, 1.0, 0.0).astype(BF16)
    tri_b = jnp.where(col >= row, 1.0, 0.0).astype(BF16)

    _hgrn_elementwise(qf_ref, ff_ref, vf_ref, _lower_bound(lbf_ref), sc_f)
    _hgrn_elementwise(qb_ref, fb_ref, vb_ref, _lower_bound(lbb_ref), sc_b)
    _hgrn_cumsum(sc_f, tri_f, n_chunks)
    _hgrn_cumsum(sc_b, tri_b, n_chunks)
    n_wide = _hgrn_factors(sc_f, n_chunks, reverse=False) + _hgrn_factors(sc_b, n_chunks, reverse=True)
    _hgrn_intra_factored(sc_f, of_ref, tri_f, n_chunks)
    _hgrn_intra_factored(sc_b, ob_ref, tri_b, n_chunks)
    _hgrn_carry_state(sc_f, of_ref, s_ref.at[0], n_chunks, reverse=False)
    _hgrn_carry_state(sc_b, ob_ref, s_ref.at[1], n_chunks, reverse=True)

    @pl.when(jnp.max(n_wide) > 0.0)
    def _():
        _hgrn_intra_halving(sc_f, of_ref, n_chunks, reverse=False)
        _hgrn_intra_halving(sc_b, ob_ref, n_chunks, reverse=True)


def _hgrn_scan(z3d, lb_fwd, lb_bwd):
    bsz, t, _ = z3d.shape
    tm = SCAN_TILE
    n_t = t // tm
    n_chunks = tm // HG_CHUNK

    def fwd(col):
        return pl.BlockSpec((None, tm, D_HG), lambda b, j: (b, j, col))

    def bwd(col):
        return pl.BlockSpec((None, tm, D_HG), lambda b, j: (b, n_t - 1 - j, col))

    lb_spec = pl.BlockSpec(lb_fwd.shape, lambda b, j: (0, 0))
    tile = lambda width, dtype: pltpu.VMEM((2, tm, width), dtype)
    scratch = _HgrnScratch(
        q=tile(D_HG, F32), k=tile(D_HG, F32), lg3=tile(3 * D_HG, BF16), b=tile(D_HG, F32), v16=tile(D_HG, BF16),
        qm=tile(D_HG, BF16), km=tile(D_HG, BF16), qi=tile(D_HG, BF16), ks=tile(D_HG, BF16),
        ds=pltpu.VMEM((2, n_chunks, HG_HEADS, HG_HEAD_DIM, HG_HEAD_DIM), F32),
        dec=pltpu.VMEM((2, n_chunks, D_HG), F32), wide=pltpu.VMEM((2, n_chunks, D_HG), F32))
    return pl.pallas_call(
        _hgrn_kernel,
        grid=(bsz, n_t),
        in_specs=[fwd(COL_Q), fwd(COL_F_FWD), fwd(COL_I), bwd(COL_Q), bwd(COL_F_BWD), bwd(COL_I),
                  lb_spec, lb_spec],
        out_specs=[pl.BlockSpec((None, tm, D_HG), lambda b, j: (b, j, 0)),
                   pl.BlockSpec((None, tm, D_HG), lambda b, j: (b, n_t - 1 - j, 0))],
        out_shape=[jax.ShapeDtypeStruct((bsz, t, D_HG), F32)] * 2,
        scratch_shapes=[pltpu.VMEM((2, HG_HEADS, HG_HEAD_DIM, HG_HEAD_DIM), F32), *scratch],
        compiler_params=pltpu.CompilerParams(
            dimension_semantics=("arbitrary", "arbitrary"), vmem_limit_bytes=VMEM_LIMIT),
        name="hgrn_scan",
    )(z3d, z3d, z3d, z3d, z3d, z3d, lb_fwd, lb_bwd)


def _softplus(x):
    return jnp.maximum(x, 0.0) + jnp.log1p(jnp.exp(-jnp.abs(x)))


def _lane_groups(width):
    return [slice(g * LANES, (g + 1) * LANES) for g in range(width // LANES)]


def _segment_pitch(seg_len):
    assert seg_len % (2 * SUBLANES) == 0
    return seg_len + SUBLANES


def _segment_rows(stage_ref, i, seg_len):
    rows = pl.ds(i, SUBLANES, stride=_segment_pitch(seg_len))
    return jnp.concatenate([stage_ref[g, rows, :] for g in range(stage_ref.shape[0])], axis=1)


def _stage_segments(stage_ref, x_ref, seg_len):
    pitch = _segment_pitch(seg_len)
    for g, lanes in enumerate(_lane_groups(x_ref.shape[1])):
        for s in range(SUBLANES):
            stage_ref[g, s * pitch:s * pitch + seg_len, :] = x_ref[s * seg_len:(s + 1) * seg_len, lanes]


def _unstage_segments(h_ref, stage_ref, seg_len):
    pitch = _segment_pitch(seg_len)
    for g, lanes in enumerate(_lane_groups(h_ref.shape[1])):
        for s in range(SUBLANES):
            h_ref[s * seg_len:(s + 1) * seg_len, lanes] = stage_ref[g, s * pitch:s * pitch + seg_len, :]


def _shift_segments(v, fill_row, *, down):
    sub = lax.broadcasted_iota(jnp.int32, v.shape, 0)
    if down:
        return jnp.where(sub >= 1, pltpu.roll(v, 1, 0), fill_row)
    return jnp.where(sub < SUBLANES - 1, pltpu.roll(v, SUBLANES - 1, 0), fill_row)


def _segmented_scan(a, u, carry, *, reverse):
    n = len(a)
    order = range(n - 1, -1, -1) if reverse else range(n)
    h, a_cum, last = [None] * n, [None] * n, None
    for i in order:
        if last is None:
            h[i], a_cum[i] = u[i], a[i]
        else:
            h[i], a_cum[i] = a[i] * h[last] + u[i], a[i] * a_cum[last]
        last = i
    h_end, a_end = h[last], a_cum[last]
    entering = [None] * SUBLANES
    for s in (range(SUBLANES - 1, -1, -1) if reverse else range(SUBLANES)):
        entering[s] = carry
        carry = h_end[s:s + 1, :] + a_end[s:s + 1, :] * carry
    entering = jnp.concatenate(entering, axis=0)
    return [h[i] + a_cum[i] * entering for i in range(n)], carry


def _rglru_direction(x_ref, prev_ref, next_ref, cw_ref, cb_ref, wa_ref, ba_ref, wx_ref, bx_ref,
                     lam_ref, h_ref, stage_ref, carry_ref, first_tile, last_tile, *, reverse):
    tm = x_ref.shape[0]
    n = tm // SUBLANES
    _stage_segments(stage_ref, x_ref, n)
    xs = [_segment_rows(stage_ref, i, n) for i in range(n)]
    prev = jnp.where(first_tile, 0.0, prev_ref[...])
    nxt = jnp.where(last_tile, 0.0, next_ref[...])
    edge = {-2: _shift_segments(xs[n - 2], prev[HALO - 2:HALO - 1, :], down=True),
            -1: _shift_segments(xs[n - 1], prev[HALO - 1:HALO, :], down=True),
            n: _shift_segments(xs[0], nxt[0:1, :], down=False)}
    tap = lambda i: xs[i] if 0 <= i < n else edge[i]
    cw = [cw_ref[w:w + 1, :] for w in range(RG_CONV_W)]
    cb = cb_ref[...]
    xc = []
    for i in range(n):
        acc = cb + tap(i - RG_CONV_LEFT) * cw[0]
        for w in range(1, RG_CONV_W):
            acc = acc + tap(i - RG_CONV_LEFT + w) * cw[w]
        xc.append(acc)
    xc = jnp.concatenate(xc, axis=0)
    xc16 = xc.astype(BF16)
    r = jax.nn.sigmoid(_dot(xc16, wa_ref[...]) + ba_ref[...])
    gate_i = jax.nn.sigmoid(_dot(xc16, wx_ref[...]) + bx_ref[...])
    log_a = (-RG_C) * r * _softplus(-lam_ref[...])
    a = jnp.exp(log_a)
    y = -jnp.tanh(log_a) * (a * a + 1.0)
    u = jnp.where(y > 0.0, y * lax.rsqrt(y), 0.0) * (gate_i * xc)
    vec = lambda v, i: v[i * SUBLANES:(i + 1) * SUBLANES, :]
    h, carry = _segmented_scan([vec(a, i) for i in range(n)], [vec(u, i) for i in range(n)], carry_ref[...],
                               reverse=reverse)
    carry_ref[...] = carry
    for i in range(n):
        for g, lanes in enumerate(_lane_groups(D_RG)):
            stage_ref[g, pl.ds(i, SUBLANES, stride=_segment_pitch(n)), :] = h[i][:, lanes]
    _unstage_segments(h_ref, stage_ref, n)


def _rglru_kernel(xf_ref, pf_ref, nf_ref, xb_ref, pb_ref, nb_ref, cw_ref, cb_ref,
                  waf_ref, baf_ref, wxf_ref, bxf_ref, lamf_ref,
                  wab_ref, bab_ref, wxb_ref, bxb_ref, lamb_ref,
                  hf_ref, hb_ref, stage_ref, carry_ref):
    j = pl.program_id(1)
    n_t = pl.num_programs(1)

    @pl.when(j == 0)
    def _():
        carry_ref[...] = jnp.zeros_like(carry_ref)

    _rglru_direction(xf_ref, pf_ref, nf_ref, cw_ref, cb_ref, waf_ref, baf_ref, wxf_ref, bxf_ref,
                     lamf_ref, hf_ref, stage_ref.at[0], carry_ref.at[0], j == 0, j == n_t - 1, reverse=False)
    _rglru_direction(xb_ref, pb_ref, nb_ref, cw_ref, cb_ref, wab_ref, bab_ref, wxb_ref, bxb_ref,
                     lamb_ref, hb_ref, stage_ref.at[1], carry_ref.at[1], j == n_t - 1, j == 0, reverse=True)


def _rglru_scan(z3d, conv_w, conv_b, gates_fwd, gates_bwd):
    bsz, t, _ = z3d.shape
    tm = SCAN_TILE
    n_t = t // tm
    halo_per_tile = tm // HALO
    n_halo = t // HALO

    def tile_specs(tile_of):
        main = pl.BlockSpec((None, tm, D_RG), lambda b, j: (b, tile_of(j), COL_RX))
        prev = pl.BlockSpec(
            (None, HALO, D_RG), lambda b, j: (b, jnp.maximum(tile_of(j) * halo_per_tile - 1, 0), COL_RX))
        nxt = pl.BlockSpec(
            (None, HALO, D_RG),
            lambda b, j: (b, jnp.minimum((tile_of(j) + 1) * halo_per_tile, n_halo - 1), COL_RX))
        return [main, prev, nxt]

    def whole(a):
        return pl.BlockSpec(a.shape, lambda b, j: (0,) * a.ndim)

    params = [conv_w, conv_b, *gates_fwd, *gates_bwd]
    return pl.pallas_call(
        _rglru_kernel,
        grid=(bsz, n_t),
        in_specs=tile_specs(lambda j: j) + tile_specs(lambda j: n_t - 1 - j) + [whole(p) for p in params],
        out_specs=[pl.BlockSpec((None, tm, D_RG), lambda b, j: (b, j, 0)),
                   pl.BlockSpec((None, tm, D_RG), lambda b, j: (b, n_t - 1 - j, 0))],
        out_shape=[jax.ShapeDtypeStruct((bsz, t, D_RG), F32)] * 2,
        scratch_shapes=[pltpu.VMEM((2, D_RG // LANES, SUBLANES * _segment_pitch(tm // SUBLANES), LANES), F32),
                        pltpu.VMEM((2, 1, D_RG), F32)],
        compiler_params=pltpu.CompilerParams(
            dimension_semantics=("arbitrary", "arbitrary"), vmem_limit_bytes=VMEM_LIMIT),
        name="rglru_scan",
    )(z3d, z3d, z3d, z3d, z3d, z3d, *params)


def _gelu_tanh(x):
    return 0.5 * x * (1.0 + jnp.tanh(0.7978845608028654 * (x + 0.044715 * (x * x * x))))


def _outproj_kernel(x_ref, mod_ref, of_ref, ob_ref, g_ref, hf_ref, hb_ref, rgate_ref, hgain_ref, w_ref, o_ref):
    gate = mod_ref[5:6, :]
    o = of_ref[...] + ob_ref[...]
    g = _silu(g_ref[...])
    heads = []
    for h in range(HG_HEADS):
        sl = slice(h * HG_HEAD_DIM, (h + 1) * HG_HEAD_DIM)
        heads.append((_rms(o[:, sl], hgain_ref[...]) * g[:, sl]).astype(BF16))
    o_rg = ((hf_ref[...] + hb_ref[...]) * _gelu_tanh(rgate_ref[...])).astype(BF16)
    y = _dot(o_rg, w_ref[D_HG:, :])
    for h in range(HG_HEADS):
        y = y + _dot(heads[h], w_ref[h * HG_HEAD_DIM:(h + 1) * HG_HEAD_DIM, :])
    o_ref[...] = x_ref[...] + gate * y


def _outproj(x2d, mod, o_f, o_b, z2d, h_f, h_b, hg_gain, w_out, *, tiles_per_batch):
    n, d = x2d.shape
    tm = ROW_TILE

    def rows(width, col=0):
        return pl.BlockSpec((tm, width), lambda i: (i, col))

    return pl.pallas_call(
        _outproj_kernel,
        grid=(n // tm,),
        in_specs=[
            rows(d),
            pl.BlockSpec((None, N_MOD, d), lambda i: (i // tiles_per_batch, 0, 0)),
            rows(D_HG), rows(D_HG), rows(D_HG, COL_G), rows(D_RG), rows(D_RG), rows(D_RG, COL_RGATE),
            _resident((1, HG_HEAD_DIM)),
            _resident(w_out.shape),
        ],
        out_specs=rows(d),
        out_shape=jax.ShapeDtypeStruct((n, d), F32),
        compiler_params=pltpu.CompilerParams(
            dimension_semantics=("arbitrary",), vmem_limit_bytes=VMEM_LIMIT),
        name="outproj",
    )(x2d, mod, o_f, o_b, z2d, h_f, h_b, z2d, hg_gain, w_out)


def _block_diag(w):
    nb, d, e = w.shape
    eye = jnp.eye(nb, dtype=w.dtype)
    return (eye[:, None, :, None] * w[:, :, None, :]).reshape(nb * d, nb * e)


def kernel(x, c, w_ada, b_ada, ffn1_norm, ffn1_w1, ffn1_w3, ffn1_w2, mix_norm, w_in, hg_lb_fwd, hg_lb_bwd, hg_out_norm, rg_conv_w, rg_conv_b, rg_wa_fwd, rg_ba_fwd, rg_wx_fwd, rg_bx_fwd, rg_lam_fwd, rg_wa_bwd, rg_ba_bwd, rg_wx_bwd, rg_bx_bwd, rg_lam_bwd, w_out, ffn2_norm, ffn2_w1, ffn2_w3, ffn2_w2, final_norm):
    bsz, t, d = x.shape
    depth = w_ada.shape[0]
    assert depth == 1 and hg_lb_fwd.shape[0] == 2, "kernels are written for a single layer"
    assert t % ROW_TILE == 0 and t % SCAN_TILE == 0 and SCAN_TILE % HG_CHUNK == 0
    tiles_per_batch = t // ROW_TILE
    row = lambda v: v.reshape(1, -1)
    l = 0

    mod = _modulation(c, w_ada[l], b_ada[l])
    x2d = x.reshape(bsz * t, d)
    fgain = row(final_norm)

    x2d = _ffn(x2d, mod, row(ffn1_norm[l]), ffn1_w1[l].astype(BF16), ffn1_w3[l].astype(BF16),
               ffn1_w2[l].astype(BF16), fgain, mod_base=0, final_norm=False, tiles_per_batch=tiles_per_batch)

    z2d = _inproj(x2d, mod, row(mix_norm[l]), w_in[l].astype(BF16), tiles_per_batch=tiles_per_batch)
    z3d = z2d.reshape(bsz, t, IN_COLS)
    o_f, o_b = _hgrn_scan(z3d, hg_lb_fwd, hg_lb_bwd)

    def gates(wa, ba, wx, bx, lam):
        return (_block_diag(wa[l]).astype(BF16), row(ba[l]), _block_diag(wx[l]).astype(BF16), row(bx[l]),
                row(lam[l]))

    h_f, h_b = _rglru_scan(z3d, rg_conv_w[l], row(rg_conv_b[l]),
                           gates(rg_wa_fwd, rg_ba_fwd, rg_wx_fwd, rg_bx_fwd, rg_lam_fwd),
                           gates(rg_wa_bwd, rg_ba_bwd, rg_wx_bwd, rg_bx_bwd, rg_lam_bwd))

    flat = lambda a: a.reshape(bsz * t, a.shape[-1])
    x2d = _outproj(x2d, mod, flat(o_f), flat(o_b), z2d, flat(h_f), flat(h_b), row(hg_out_norm[l]),
                   w_out[l].astype(BF16), tiles_per_batch=tiles_per_batch)

    out = _ffn(x2d, mod, row(ffn2_norm[l]), ffn2_w1[l].astype(BF16), ffn2_w3[l].astype(BF16),
               ffn2_w2[l].astype(BF16), fgain, mod_base=6, final_norm=True, tiles_per_batch=tiles_per_batch)
    return out.reshape(bsz, t, d)
```

```python
import math
from typing import NamedTuple

import jax
import jax.numpy as jnp
from jax import lax
from jax.experimental import pallas as pl
from jax.experimental.pallas import tpu as pltpu

F32 = jnp.float32
BF16 = jnp.bfloat16

EPS = 1e-6
N_MOD = 9
HG_HEAD_DIM = 128
HG_HEADS = 4
D_HG = HG_HEADS * HG_HEAD_DIM
RG_BLOCKS = 8
D_RG = 512
RG_CONV_W = 4
RG_CONV_LEFT = RG_CONV_W // 2
RG_C = 8.0
IN_COLS = 5 * D_HG + 2 * D_RG
COL_Q, COL_F_FWD, COL_F_BWD, COL_I, COL_G, COL_RX, COL_RGATE = range(7)

SUBLANES = 8
LANES = 128
HALO = SUBLANES
HG_CHUNK = 64
LOG2_E = math.log2(math.e)
HG_SAFE_EXPONENT_LOG2 = 115.0
VMEM_LIMIT = 56 * 1024 * 1024

ROW_TILE = 256
SCAN_TILE = 256


def _rms(x, gain):
    return x * lax.rsqrt(jnp.mean(x * x, axis=-1, keepdims=True) + EPS) * gain


def _norm_mod(x, gain, shift, scale):
    return _rms(x, gain) * (1.0 + scale) + shift


def _silu(x):
    return x * jax.nn.sigmoid(x)


def _dot(a, b):
    return jnp.dot(a, b, preferred_element_type=F32)


def _dot_nt(a, b):
    return lax.dot_general(a, b, (((1,), (1,)), ((), ())), preferred_element_type=F32)


def _dot_tn(a, b):
    return lax.dot_general(a, b, (((0,), (0,)), ((), ())), preferred_element_type=F32)


def _mod_kernel(c_ref, w_ref, b_ref, o_ref):
    o_ref[...] = _dot(_silu(c_ref[...]), w_ref[...]) + b_ref[...]


def _modulation(c, w_ada, b_ada):
    bsz, d = c.shape
    n = w_ada.shape[1]
    bn = 1024
    c_pad = jnp.pad(c, ((0, SUBLANES - bsz), (0, 0)))
    out = pl.pallas_call(
        _mod_kernel,
        grid=(n // bn,),
        in_specs=[
            pl.BlockSpec((SUBLANES, d), lambda j: (0, 0)),
            pl.BlockSpec((d, bn), lambda j: (0, j)),
            pl.BlockSpec((1, bn), lambda j: (0, j)),
        ],
        out_specs=pl.BlockSpec((SUBLANES, bn), lambda j: (0, j)),
        out_shape=jax.ShapeDtypeStruct((SUBLANES, n), F32),
        name="modulation",
    )(c_pad, w_ada, b_ada.reshape(1, n))
    return out[:bsz].reshape(bsz, N_MOD, d)


def _swiglu_residual(x, mod_ref, mod_base, gain_ref, w1_ref, w3_ref, w2_ref):
    shift = mod_ref[mod_base:mod_base + 1, :]
    scale = mod_ref[mod_base + 1:mod_base + 2, :]
    gate = mod_ref[mod_base + 2:mod_base + 3, :]
    h = _norm_mod(x, gain_ref[...], shift, scale).astype(BF16)
    a = _dot(h, w1_ref[...])
    b = _dot(h, w3_ref[...])
    g = (_silu(a) * b).astype(BF16)
    return x + (0.5 * gate) * _dot(g, w2_ref[...])


def _ffn_inproj_kernel(x_ref, mod_ref, gain_ref, w1_ref, w3_ref, w2_ref, mix_gain_ref, win_ref, x1_ref, z_ref):
    x1 = _swiglu_residual(x_ref[...], mod_ref, 0, gain_ref, w1_ref, w3_ref, w2_ref)
    x1_ref[...] = x1
    h = _norm_mod(x1, mix_gain_ref[...], mod_ref[3:4, :], mod_ref[4:5, :]).astype(BF16)
    z_ref[...] = _dot(h, win_ref[...])


def _resident(shape):
    return pl.BlockSpec(shape, lambda *_: (0,) * len(shape), pipeline_mode=pl.Buffered(1))


def _row_tiles(tm, width, col=0):
    return pl.BlockSpec((tm, width), lambda i: (i, col))


def _mod_rows(d, tiles_per_batch):
    return pl.BlockSpec((None, N_MOD, d), lambda i: (i // tiles_per_batch, 0, 0))


def _ffn_inproj(x2d, mod, gain, w1, w3, w2, mix_gain, w_in, *, tiles_per_batch):
    n, d = x2d.shape
    cols = w_in.shape[1]
    tm = ROW_TILE
    return pl.pallas_call(
        _ffn_inproj_kernel,
        grid=(n // tm,),
        in_specs=[_row_tiles(tm, d), _mod_rows(d, tiles_per_batch), _resident(gain.shape), _resident(w1.shape),
                  _resident(w3.shape), _resident(w2.shape), _resident(mix_gain.shape), _resident(w_in.shape)],
        out_specs=[_row_tiles(tm, d), _row_tiles(tm, cols)],
        out_shape=[jax.ShapeDtypeStruct((n, d), F32), jax.ShapeDtypeStruct((n, cols), F32)],
        compiler_params=pltpu.CompilerParams(
            dimension_semantics=("arbitrary",), vmem_limit_bytes=VMEM_LIMIT),
        name="ffn_inproj",
    )(x2d, mod, gain, w1, w3, w2, mix_gain, w_in)


def _lower_bound(lb_ref):
    p0 = lb_ref[0:1, :]
    p1 = lb_ref[1:2, :]
    m = jnp.maximum(p0, p1)
    e0 = jnp.exp(p0 - m)
    e1 = jnp.exp(p1 - m)
    return e1 / (e0 + e1)


def _split3(x):
    hi = x.astype(BF16)
    r = x - hi.astype(F32)
    mid = r.astype(BF16)
    lo = (r - mid.astype(F32)).astype(BF16)
    return hi, mid, lo


class _HgrnScratch(NamedTuple):
    q: object
    k: object
    lg3: object
    b: object
    v16: object
    qm: object
    km: object
    qi: object
    ks: object
    ds: object
    dec: object
    wide: object


def _chunk_rows(c):
    return slice(c * HG_CHUNK, (c + 1) * HG_CHUNK)


def _head_lanes(h):
    return slice(h * HG_HEAD_DIM, (h + 1) * HG_HEAD_DIM)


def _chunk_marks(reverse):
    c = HG_CHUNK
    return (c - 1, 0, c // 2) if reverse else (0, c - 1, c // 2 - 1)


def _hgrn_elementwise(q_ref, f_ref, v_ref, lb, sc):
    sc.q[...] = _silu(q_ref[...])
    f = lb + (1.0 - lb) * jax.nn.sigmoid(f_ref[...])
    sc.k[...] = 1.0 - f
    hi, mid, lo = _split3(jnp.log(f))
    sc.lg3[:, 0:D_HG] = hi
    sc.lg3[:, D_HG:2 * D_HG] = mid
    sc.lg3[:, 2 * D_HG:3 * D_HG] = lo
    sc.v16[...] = v_ref[...].astype(BF16)


def _hgrn_cumsum(sc, tri, n_chunks):
    for c in range(n_chunks):
        rows = _chunk_rows(c)
        p = _dot(tri, sc.lg3[rows, :])
        sc.b[rows, :] = p[:, 0:D_HG] + p[:, D_HG:2 * D_HG] + p[:, 2 * D_HG:3 * D_HG]


def _hgrn_factors(sc, n_chunks, *, reverse):
    first, last, ref = _chunk_marks(reverse)
    n_wide = None
    for c in range(n_chunks):
        rows = _chunk_rows(c)
        b2 = sc.b[rows, :] * LOG2_E
        r2 = b2[ref:ref + 1, :]
        end2 = b2[last:last + 1, :]
        spread = jnp.maximum(b2[first:first + 1, :] - r2, r2 - end2)
        wide = jnp.where(spread < HG_SAFE_EXPONENT_LOG2, 0.0, 1.0)
        sc.wide[c:c + 1, :] = wide
        n_wide = wide if n_wide is None else n_wide + wide
        narrow = jnp.broadcast_to(wide, b2.shape) == 0.0
        q, k = sc.q[rows, :], sc.k[rows, :]
        sc.qm[rows, :] = jnp.where(narrow, q * jnp.exp2(b2 - r2), 0.0).astype(BF16)
        sc.km[rows, :] = jnp.where(narrow, k * jnp.exp2(r2 - b2), 0.0).astype(BF16)
        sc.qi[rows, :] = (q * jnp.exp2(b2)).astype(BF16)
        sc.ks[rows, :] = (k * jnp.exp2(end2 - b2)).astype(BF16)
        sc.dec[c:c + 1, :] = jnp.exp2(end2)
    return n_wide


def _hgrn_state_increment(sc, c, h):
    rows, sl = _chunk_rows(c), _head_lanes(h)
    sc.ds[c, h] = _dot_tn(sc.v16[rows, sl], sc.ks[rows, sl])


def _hgrn_intra_factored(sc, o_ref, tri, n_chunks):
    causal = tri > 0
    pairs = [(c, h) for c in range(n_chunks) for h in range(HG_HEADS)]
    for c, h in pairs:
        _hgrn_state_increment(sc, c, h)
    scores = {}
    for c, h in pairs:
        rows, sl = _chunk_rows(c), _head_lanes(h)
        scores[c, h] = jnp.where(causal, _dot_nt(sc.qm[rows, sl], sc.km[rows, sl]), 0.0).astype(BF16)
    for c, h in pairs:
        rows, sl = _chunk_rows(c), _head_lanes(h)
        o_ref[rows, sl] = _dot(scores[c, h], sc.v16[rows, sl])


def _hgrn_intra_halving(sc, o_ref, n_chunks, *, reverse):
    cs = HG_CHUNK
    row = lax.broadcasted_iota(jnp.int32, (cs, cs), 0)
    col = lax.broadcasted_iota(jnp.int32, (cs, cs), 1)
    pos = lax.broadcasted_iota(jnp.int32, (cs, D_HG), 0)
    for c in range(n_chunks):
        rows = _chunk_rows(c)
        wide = jnp.broadcast_to(sc.wide[c:c + 1, :], (cs, D_HG)) > 0.0
        b, k = sc.b[rows, :], sc.k[rows, :]
        q = jnp.where(wide, sc.q[rows, :], 0.0)
        q16, k16 = q.astype(BF16), k.astype(BF16)
        scores = [jnp.where(row == col, _dot_nt(q16[:, _head_lanes(h)], k16[:, _head_lanes(h)]), 0.0)
                  for h in range(HG_HEADS)]
        b_hi, b_mid, b_lo = _split3(b)
        shift = 1
        while (1 << shift) <= cs:
            blk, half = 1 << shift, 1 << (shift - 1)
            boundary = ((row >> shift) << shift) + (half if reverse else half - 1)
            pick = jnp.where(col == boundary, 1.0, 0.0).astype(BF16)
            b_bnd = _dot(pick, b_hi) + _dot(pick, b_mid) + _dot(pick, b_lo)
            in_half = (pos & (blk - 1)) >= half
            attends = jnp.logical_not(in_half) if reverse else in_half
            qh = jnp.where(attends, q * jnp.exp(b - b_bnd), 0.0).astype(BF16)
            kh = jnp.where(attends, 0.0, k * jnp.exp(b_bnd - b)).astype(BF16)
            same_block = (row >> shift) == (col >> shift)
            for h in range(HG_HEADS):
                sl = _head_lanes(h)
                scores[h] = scores[h] + jnp.where(same_block, _dot_nt(qh[:, sl], kh[:, sl]), 0.0)
            shift += 1
        for h in range(HG_HEADS):
            sl = _head_lanes(h)
            o_ref[rows, sl] += _dot(scores[h].astype(BF16), sc.v16[rows, sl])


def _hgrn_carry_state(sc, o_ref, s_ref, n_chunks, *, reverse):
    order = range(n_chunks - 1, -1, -1) if reverse else range(n_chunks)
    for c in order:
        rows = _chunk_rows(c)
        for h in range(HG_HEADS):
            sl = _head_lanes(h)
            st = s_ref[h]
            o_ref[rows, sl] += _dot_nt(sc.qi[rows, sl], st.astype(BF16))
            s_ref[h] = st * sc.dec[c:c + 1, sl] + sc.ds[c, h]


def _hgrn_kernel(qf_ref, ff_ref, vf_ref, qb_ref, fb_ref, vb_ref, lbf_ref, lbb_ref, of_ref, ob_ref, s_ref, *scratch):
    tm = of_ref.shape[0]
    n_chunks = tm // HG_CHUNK
    sc_f = _HgrnScratch(*(r.at[0] for r in scratch))
    sc_b = _HgrnScratch(*(r.at[1] for r in scratch))

    @pl.when(pl.program_id(1) == 0)
    def _():
        s_ref[...] = jnp.zeros_like(s_ref)

    row = lax.broadcasted_iota(jnp.int32, (HG_CHUNK, HG_CHUNK), 0)
    col = lax.broadcasted_iota(jnp.int32, (HG_CHUNK, HG_CHUNK), 1)
    tri_f = jnp.where(col <= row, 1.0, 0.0).astype(BF16)
    tri_b = jnp.where(col >= row, 1.0, 0.0).astype(BF16)

    _hgrn_elementwise(qf_ref, ff_ref, vf_ref, _lower_bound(lbf_ref), sc_f)
    _hgrn_elementwise(qb_ref, fb_ref, vb_ref, _lower_bound(lbb_ref), sc_b)
    _hgrn_cumsum(sc_f, tri_f, n_chunks)
    _hgrn_cumsum(sc_b, tri_b, n_chunks)
    n_wide = _hgrn_factors(sc_f, n_chunks, reverse=False) + _hgrn_factors(sc_b, n_chunks, reverse=True)
    _hgrn_intra_factored(sc_f, of_ref, tri_f, n_chunks)
    _hgrn_intra_factored(sc_b, ob_ref, tri_b, n_chunks)
    _hgrn_carry_state(sc_f, of_ref, s_ref.at[0], n_chunks, reverse=False)
    _hgrn_carry_state(sc_b, ob_ref, s_ref.at[1], n_chunks, reverse=True)

    @pl.when(jnp.max(n_wide) > 0.0)
    def _():
        _hgrn_intra_halving(sc_f, of_ref, n_chunks, reverse=False)
        _hgrn_intra_halving(sc_b, ob_ref, n_chunks, reverse=True)


def _hgrn_scan(z3d, lb_fwd, lb_bwd):
    bsz, t, _ = z3d.shape
    tm = SCAN_TILE
    n_t = t // tm
    n_chunks = tm // HG_CHUNK

    def fwd(col):
        return pl.BlockSpec((None, tm, D_HG), lambda b, j: (b, j, col))

    def bwd(col):
        return pl.BlockSpec((None, tm, D_HG), lambda b, j: (b, n_t - 1 - j, col))

    lb_spec = pl.BlockSpec(lb_fwd.shape, lambda b, j: (0, 0))
    tile = lambda width, dtype: pltpu.VMEM((2, tm, width), dtype)
    scratch = _HgrnScratch(
        q=tile(D_HG, F32), k=tile(D_HG, F32), lg3=tile(3 * D_HG, BF16), b=tile(D_HG, F32), v16=tile(D_HG, BF16),
        qm=tile(D_HG, BF16), km=tile(D_HG, BF16), qi=tile(D_HG, BF16), ks=tile(D_HG, BF16),
        ds=pltpu.VMEM((2, n_chunks, HG_HEADS, HG_HEAD_DIM, HG_HEAD_DIM), F32),
        dec=pltpu.VMEM((2, n_chunks, D_HG), F32), wide=pltpu.VMEM((2, n_chunks, D_HG), F32))
    return pl.pallas_call(
        _hgrn_kernel,
        grid=(bsz, n_t),
        in_specs=[fwd(COL_Q), fwd(COL_F_FWD), fwd(COL_I), bwd(COL_Q), bwd(COL_F_BWD), bwd(COL_I),
                  lb_spec, lb_spec],
        out_specs=[pl.BlockSpec((None, tm, D_HG), lambda b, j: (b, j, 0)),
                   pl.BlockSpec((None, tm, D_HG), lambda b, j: (b, n_t - 1 - j, 0))],
        out_shape=[jax.ShapeDtypeStruct((bsz, t, D_HG), F32)] * 2,
        scratch_shapes=[pltpu.VMEM((2, HG_HEADS, HG_HEAD_DIM, HG_HEAD_DIM), F32), *scratch],
        compiler_params=pltpu.CompilerParams(
            dimension_semantics=("arbitrary", "arbitrary"), vmem_limit_bytes=VMEM_LIMIT),
        name="hgrn_scan",
    )(z3d, z3d, z3d, z3d, z3d, z3d, lb_fwd, lb_bwd)


def _softplus(x):
    return jnp.maximum(x, 0.0) + jnp.log1p(jnp.exp(-jnp.abs(x)))


def _lane_groups(width):
    return [slice(g * LANES, (g + 1) * LANES) for g in range(width // LANES)]


def _segment_pitch(seg_len):
    assert seg_len % (2 * SUBLANES) == 0
    return seg_len + SUBLANES


def _segment_rows(stage_ref, i, seg_len):
    rows = pl.ds(i, SUBLANES, stride=_segment_pitch(seg_len))
    return jnp.concatenate([stage_ref[g, rows, :] for g in range(stage_ref.shape[0])], axis=1)


def _stage_segments(stage_ref, x_ref, seg_len):
    pitch = _segment_pitch(seg_len)
    for g, lanes in enumerate(_lane_groups(x_ref.shape[1])):
        for s in range(SUBLANES):
            stage_ref[g, s * pitch:s * pitch + seg_len, :] = x_ref[s * seg_len:(s + 1) * seg_len, lanes]


def _unstage_segments(h_ref, stage_ref, seg_len):
    pitch = _segment_pitch(seg_len)
    for g, lanes in enumerate(_lane_groups(h_ref.shape[1])):
        for s in range(SUBLANES):
            h_ref[s * seg_len:(s + 1) * seg_len, lanes] = stage_ref[g, s * pitch:s * pitch + seg_len, :]


def _shift_segments(v, fill_row, *, down):
    sub = lax.broadcasted_iota(jnp.int32, v.shape, 0)
    if down:
        return jnp.where(sub >= 1, pltpu.roll(v, 1, 0), fill_row)
    return jnp.where(sub < SUBLANES - 1, pltpu.roll(v, SUBLANES - 1, 0), fill_row)


def _segmented_scan(a, u, carry, *, reverse):
    n = len(a)
    order = range(n - 1, -1, -1) if reverse else range(n)
    h, a_cum, last = [None] * n, [None] * n, None
    for i in order:
        if last is None:
            h[i], a_cum[i] = u[i], a[i]
        else:
            h[i], a_cum[i] = a[i] * h[last] + u[i], a[i] * a_cum[last]
        last = i
    h_end, a_end = h[last], a_cum[last]
    entering = [None] * SUBLANES
    for s in (range(SUBLANES - 1, -1, -1) if reverse else range(SUBLANES)):
        entering[s] = carry
        carry = h_end[s:s + 1, :] + a_end[s:s + 1, :] * carry
    entering = jnp.concatenate(entering, axis=0)
    return [h[i] + a_cum[i] * entering for i in range(n)], carry


def _rglru_direction(x_ref, prev_ref, next_ref, cw_ref, cb_ref, wa_ref, ba_ref, wx_ref, bx_ref,
                     lam_ref, h_ref, stage_ref, carry_ref, first_tile, last_tile, *, reverse):
    tm = x_ref.shape[0]
    n = tm // SUBLANES
    _stage_segments(stage_ref, x_ref, n)
    xs = [_segment_rows(stage_ref, i, n) for i in range(n)]
    prev = jnp.where(first_tile, 0.0, prev_ref[...])
    nxt = jnp.where(last_tile, 0.0, next_ref[...])
    edge = {-2: _shift_segments(xs[n - 2], prev[HALO - 2:HALO - 1, :], down=True),
            -1: _shift_segments(xs[n - 1], prev[HALO - 1:HALO, :], down=True),
            n: _shift_segments(xs[0], nxt[0:1, :], down=False)}
    tap = lambda i: xs[i] if 0 <= i < n else edge[i]
    cw = [cw_ref[w:w + 1, :] for w in range(RG_CONV_W)]
    cb = cb_ref[...]
    xc = []
    for i in range(n):
        acc = cb + tap(i - RG_CONV_LEFT) * cw[0]
        for w in range(1, RG_CONV_W):
            acc = acc + tap(i - RG_CONV_LEFT + w) * cw[w]
        xc.append(acc)
    xc = jnp.concatenate(xc, axis=0)
    xc16 = xc.astype(BF16)
    r = jax.nn.sigmoid(_dot(xc16, wa_ref[...]) + ba_ref[...])
    gate_i = jax.nn.sigmoid(_dot(xc16, wx_ref[...]) + bx_ref[...])
    log_a = (-RG_C) * r * _softplus(-lam_ref[...])
    a = jnp.exp(log_a)
    y = -jnp.tanh(log_a) * (a * a + 1.0)
    u = jnp.where(y > 0.0, y * lax.rsqrt(y), 0.0) * (gate_i * xc)
    vec = lambda v, i: v[i * SUBLANES:(i + 1) * SUBLANES, :]
    h, carry = _segmented_scan([vec(a, i) for i in range(n)], [vec(u, i) for i in range(n)], carry_ref[...],
                               reverse=reverse)
    carry_ref[...] = carry
    for i in range(n):
        for g, lanes in enumerate(_lane_groups(D_RG)):
            stage_ref[g, pl.ds(i, SUBLANES, stride=_segment_pitch(n)), :] = h[i][:, lanes]
    _unstage_segments(h_ref, stage_ref, n)


def _rglru_kernel(xf_ref, pf_ref, nf_ref, xb_ref, pb_ref, nb_ref, cw_ref, cb_ref,
                  waf_ref, baf_ref, wxf_ref, bxf_ref, lamf_ref,
                  wab_ref, bab_ref, wxb_ref, bxb_ref, lamb_ref,
                  hf_ref, hb_ref, stage_ref, carry_ref):
    j = pl.program_id(1)
    n_t = pl.num_programs(1)

    @pl.when(j == 0)
    def _():
        carry_ref[...] = jnp.zeros_like(carry_ref)

    _rglru_direction(xf_ref, pf_ref, nf_ref, cw_ref, cb_ref, waf_ref, baf_ref, wxf_ref, bxf_ref,
                     lamf_ref, hf_ref, stage_ref.at[0], carry_ref.at[0], j == 0, j == n_t - 1, reverse=False)
    _rglru_direction(xb_ref, pb_ref, nb_ref, cw_ref, cb_ref, wab_ref, bab_ref, wxb_ref, bxb_ref,
                     lamb_ref, hb_ref, stage_ref.at[1], carry_ref.at[1], j == n_t - 1, j == 0, reverse=True)


def _rglru_scan(z3d, conv_w, conv_b, gates_fwd, gates_bwd):
    bsz, t, _ = z3d.shape
    tm = SCAN_TILE
    n_t = t // tm
    halo_per_tile = tm // HALO
    n_halo = t // HALO

    def tile_specs(tile_of):
        main = pl.BlockSpec((None, tm, D_RG), lambda b, j: (b, tile_of(j), COL_RX))
        prev = pl.BlockSpec(
            (None, HALO, D_RG), lambda b, j: (b, jnp.maximum(tile_of(j) * halo_per_tile - 1, 0), COL_RX))
        nxt = pl.BlockSpec(
            (None, HALO, D_RG),
            lambda b, j: (b, jnp.minimum((tile_of(j) + 1) * halo_per_tile, n_halo - 1), COL_RX))
        return [main, prev, nxt]

    def whole(a):
        return pl.BlockSpec(a.shape, lambda b, j: (0,) * a.ndim)

    params = [conv_w, conv_b, *gates_fwd, *gates_bwd]
    return pl.pallas_call(
        _rglru_kernel,
        grid=(bsz, n_t),
        in_specs=tile_specs(lambda j: j) + tile_specs(lambda j: n_t - 1 - j) + [whole(p) for p in params],
        out_specs=[pl.BlockSpec((None, tm, D_RG), lambda b, j: (b, j, 0)),
                   pl.BlockSpec((None, tm, D_RG), lambda b, j: (b, n_t - 1 - j, 0))],
        out_shape=[jax.ShapeDtypeStruct((bsz, t, D_RG), F32)] * 2,
        scratch_shapes=[pltpu.VMEM((2, D_RG // LANES, SUBLANES * _segment_pitch(tm // SUBLANES), LANES), F32),
                        pltpu.VMEM((2, 1, D_RG), F32)],
        compiler_params=pltpu.CompilerParams(
            dimension_semantics=("arbitrary", "arbitrary"), vmem_limit_bytes=VMEM_LIMIT),
        name="rglru_scan",
    )(z3d, z3d, z3d, z3d, z3d, z3d, *params)


def _gelu_tanh(x):
    return 0.5 * x * (1.0 + jnp.tanh(0.7978845608028654 * (x + 0.044715 * (x * x * x))))


def _mix_residual(x, gate, of_ref, ob_ref, g_ref, hf_ref, hb_ref, rgate_ref, hgain_ref, w_ref):
    o = of_ref[...] + ob_ref[...]
    g = _silu(g_ref[...])
    o_rg = ((hf_ref[...] + hb_ref[...]) * _gelu_tanh(rgate_ref[...])).astype(BF16)
    y = _dot(o_rg, w_ref[D_HG:, :])
    for h in range(HG_HEADS):
        sl = _head_lanes(h)
        o_hg = (_rms(o[:, sl], hgain_ref[...]) * g[:, sl]).astype(BF16)
        y = y + _dot(o_hg, w_ref[sl, :])
    return x + gate * y


def _outproj_ffn_kernel(x_ref, mod_ref, of_ref, ob_ref, g_ref, hf_ref, hb_ref, rgate_ref, hgain_ref, wout_ref,
                        gain_ref, w1_ref, w3_ref, w2_ref, fgain_ref, o_ref):
    x2 = _mix_residual(x_ref[...], mod_ref[5:6, :], of_ref, ob_ref, g_ref, hf_ref, hb_ref, rgate_ref, hgain_ref,
                       wout_ref)
    x3 = _swiglu_residual(x2, mod_ref, 6, gain_ref, w1_ref, w3_ref, w2_ref)
    o_ref[...] = _rms(x3, fgain_ref[...])


def _outproj_ffn(x2d, mod, o_f, o_b, z2d, h_f, h_b, hg_gain, w_out, gain, w1, w3, w2, final_gain, *,
                 tiles_per_batch):
    n, d = x2d.shape
    tm = ROW_TILE
    return pl.pallas_call(
        _outproj_ffn_kernel,
        grid=(n // tm,),
        in_specs=[_row_tiles(tm, d), _mod_rows(d, tiles_per_batch),
                  _row_tiles(tm, D_HG), _row_tiles(tm, D_HG), _row_tiles(tm, D_HG, COL_G),
                  _row_tiles(tm, D_RG), _row_tiles(tm, D_RG), _row_tiles(tm, D_RG, COL_RGATE),
                  _resident(hg_gain.shape), _resident(w_out.shape), _resident(gain.shape),
                  _resident(w1.shape), _resident(w3.shape), _resident(w2.shape), _resident(final_gain.shape)],
        out_specs=_row_tiles(tm, d),
        out_shape=jax.ShapeDtypeStruct((n, d), F32),
        compiler_params=pltpu.CompilerParams(
            dimension_semantics=("arbitrary",), vmem_limit_bytes=VMEM_LIMIT),
        name="outproj_ffn",
    )(x2d, mod, o_f, o_b, z2d, h_f, h_b, z2d, hg_gain, w_out, gain, w1, w3, w2, final_gain)


def _block_diag(w):
    nb, d, e = w.shape
    eye = jnp.eye(nb, dtype=w.dtype)
    return (eye[:, None, :, None] * w[:, :, None, :]).reshape(nb * d, nb * e)


def kernel(x, c, w_ada, b_ada, ffn1_norm, ffn1_w1, ffn1_w3, ffn1_w2, mix_norm, w_in, hg_lb_fwd, hg_lb_bwd, hg_out_norm, rg_conv_w, rg_conv_b, rg_wa_fwd, rg_ba_fwd, rg_wx_fwd, rg_bx_fwd, rg_lam_fwd, rg_wa_bwd, rg_ba_bwd, rg_wx_bwd, rg_bx_bwd, rg_lam_bwd, w_out, ffn2_norm, ffn2_w1, ffn2_w3, ffn2_w2, final_norm):
    bsz, t, d = x.shape
    depth = w_ada.shape[0]
    assert depth == 1 and hg_lb_fwd.shape[0] == 2, "kernels are written for a single layer"
    assert t % ROW_TILE == 0 and t % SCAN_TILE == 0 and SCAN_TILE % HG_CHUNK == 0
    tiles_per_batch = t // ROW_TILE
    row = lambda v: v.reshape(1, -1)
    l = 0

    mod = _modulation(c, w_ada[l], b_ada[l])
    bf16 = lambda w: w[l].astype(BF16)
    flat = lambda a: a.reshape(bsz * t, a.shape[-1])

    x1, z2d = _ffn_inproj(flat(x), mod, row(ffn1_norm[l]), bf16(ffn1_w1), bf16(ffn1_w3), bf16(ffn1_w2),
                          row(mix_norm[l]), bf16(w_in), tiles_per_batch=tiles_per_batch)
    z3d = z2d.reshape(bsz, t, IN_COLS)
    o_f, o_b = _hgrn_scan(z3d, hg_lb_fwd, hg_lb_bwd)

    def gates(wa, ba, wx, bx, lam):
        return (_block_diag(wa[l]).astype(BF16), row(ba[l]), _block_diag(wx[l]).astype(BF16), row(bx[l]),
                row(lam[l]))

    h_f, h_b = _rglru_scan(z3d, rg_conv_w[l], row(rg_conv_b[l]),
                           gates(rg_wa_fwd, rg_ba_fwd, rg_wx_fwd, rg_bx_fwd, rg_lam_fwd),
                           gates(rg_wa_bwd, rg_ba_bwd, rg_wx_bwd, rg_bx_bwd, rg_lam_bwd))

    out = _outproj_ffn(x1, mod, flat(o_f), flat(o_b), z2d, flat(h_f), flat(h_b), row(hg_out_norm[l]), bf16(w_out),
                       row(ffn2_norm[l]), bf16(ffn2_w1), bf16(ffn2_w3), bf16(ffn2_w2), row(final_norm),
                       tiles_per_batch=tiles_per_batch)
    return out.reshape(bsz, t, d)
```

```python
import math
from typing import NamedTuple

import jax
import jax.numpy as jnp
from jax import lax
from jax.experimental import pallas as pl
from jax.experimental.pallas import tpu as pltpu

F32 = jnp.float32
BF16 = jnp.bfloat16

EPS = 1e-6
N_MOD = 9
HG_HEAD_DIM = 128
HG_HEADS = 4
D_HG = HG_HEADS * HG_HEAD_DIM
RG_BLOCKS = 8
D_RG = 512
RG_CONV_W = 4
RG_CONV_LEFT = RG_CONV_W // 2
RG_C = 8.0
IN_COLS = 5 * D_HG + 2 * D_RG
COL_Q, COL_F_FWD, COL_F_BWD, COL_I, COL_G, COL_RX, COL_RGATE = range(7)

SUBLANES = 8
LANES = 128
HALO = SUBLANES
HG_CHUNK = 64
LOG2_E = math.log2(math.e)
HG_SAFE_EXPONENT_LOG2 = 115.0
VMEM_LIMIT = 56 * 1024 * 1024

ROW_TILE = 256
SCAN_TILE = 256


def _rms(x, gain):
    return x * lax.rsqrt(jnp.mean(x * x, axis=-1, keepdims=True) + EPS) * gain


def _norm_mod(x, gain, shift, scale):
    return _rms(x, gain) * (1.0 + scale) + shift


def _silu(x):
    return x * jax.nn.sigmoid(x)


def _dot(a, b):
    return jnp.dot(a, b, preferred_element_type=F32)


def _dot_nt(a, b):
    return lax.dot_general(a, b, (((1,), (1,)), ((), ())), preferred_element_type=F32)


def _dot_tn(a, b):
    return lax.dot_general(a, b, (((0,), (0,)), ((), ())), preferred_element_type=F32)


def _mod_kernel(c_ref, w_ref, b_ref, o_ref):
    o_ref[...] = _dot(_silu(c_ref[...]), w_ref[...]) + b_ref[...]


def _modulation(c, w_ada, b_ada):
    bsz, d = c.shape
    n = w_ada.shape[1]
    bn = 1024
    c_pad = jnp.pad(c, ((0, SUBLANES - bsz), (0, 0)))
    out = pl.pallas_call(
        _mod_kernel,
        grid=(n // bn,),
        in_specs=[
            pl.BlockSpec((SUBLANES, d), lambda j: (0, 0)),
            pl.BlockSpec((d, bn), lambda j: (0, j)),
            pl.BlockSpec((1, bn), lambda j: (0, j)),
        ],
        out_specs=pl.BlockSpec((SUBLANES, bn), lambda j: (0, j)),
        out_shape=jax.ShapeDtypeStruct((SUBLANES, n), F32),
        name="modulation",
    )(c_pad, w_ada, b_ada.reshape(1, n))
    return out[:bsz].reshape(bsz, N_MOD, d)


def _swiglu_residual(x, mod_ref, mod_base, gain_ref, w1_ref, w3_ref, w2_ref):
    shift = mod_ref[mod_base:mod_base + 1, :]
    scale = mod_ref[mod_base + 1:mod_base + 2, :]
    gate = mod_ref[mod_base + 2:mod_base + 3, :]
    h = _norm_mod(x, gain_ref[...], shift, scale).astype(BF16)
    a = _dot(h, w1_ref[...])
    b = _dot(h, w3_ref[...])
    g = (_silu(a) * b).astype(BF16)
    return x + (0.5 * gate) * _dot(g, w2_ref[...])


def _ffn_inproj_kernel(x_ref, mod_ref, gain_ref, w1_ref, w3_ref, w2_ref, mix_gain_ref, win_ref, x1_ref, z_ref):
    x1 = _swiglu_residual(x_ref[...], mod_ref, 0, gain_ref, w1_ref, w3_ref, w2_ref)
    x1_ref[...] = x1
    h = _norm_mod(x1, mix_gain_ref[...], mod_ref[3:4, :], mod_ref[4:5, :]).astype(BF16)
    z_ref[...] = _dot(h, win_ref[...])


def _resident(shape):
    return pl.BlockSpec(shape, lambda *_: (0,) * len(shape), pipeline_mode=pl.Buffered(1))


def _row_tiles(tm, width, col=0):
    return pl.BlockSpec((tm, width), lambda i: (i, col))


def _mod_rows(d, tiles_per_batch):
    return pl.BlockSpec((None, N_MOD, d), lambda i: (i // tiles_per_batch, 0, 0))


def _ffn_inproj(x2d, mod, gain, w1, w3, w2, mix_gain, w_in, *, tiles_per_batch):
    n, d = x2d.shape
    cols = w_in.shape[1]
    tm = ROW_TILE
    return pl.pallas_call(
        _ffn_inproj_kernel,
        grid=(n // tm,),
        in_specs=[_row_tiles(tm, d), _mod_rows(d, tiles_per_batch), _resident(gain.shape), _resident(w1.shape),
                  _resident(w3.shape), _resident(w2.shape), _resident(mix_gain.shape), _resident(w_in.shape)],
        out_specs=[_row_tiles(tm, d), _row_tiles(tm, cols)],
        out_shape=[jax.ShapeDtypeStruct((n, d), F32), jax.ShapeDtypeStruct((n, cols), F32)],
        compiler_params=pltpu.CompilerParams(
            dimension_semantics=("arbitrary",), vmem_limit_bytes=VMEM_LIMIT),
        name="ffn_inproj",
    )(x2d, mod, gain, w1, w3, w2, mix_gain, w_in)


def _lower_bound(lb_ref):
    p0 = lb_ref[0:1, :]
    p1 = lb_ref[1:2, :]
    m = jnp.maximum(p0, p1)
    e0 = jnp.exp(p0 - m)
    e1 = jnp.exp(p1 - m)
    return e1 / (e0 + e1)


def _split3(x):
    hi = x.astype(BF16)
    r = x - hi.astype(F32)
    mid = r.astype(BF16)
    lo = (r - mid.astype(F32)).astype(BF16)
    return hi, mid, lo


class _HgrnScratch(NamedTuple):
    q: object
    k: object
    lg3: object
    b: object
    v16: object
    qm: object
    km: object
    qi: object
    ks: object
    ds: object
    dec: object
    wide: object


def _chunk_rows(c):
    return slice(c * HG_CHUNK, (c + 1) * HG_CHUNK)


def _head_lanes(h):
    return slice(h * HG_HEAD_DIM, (h + 1) * HG_HEAD_DIM)


def _chunk_marks(reverse):
    c = HG_CHUNK
    return (c - 1, 0, c // 2) if reverse else (0, c - 1, c // 2 - 1)


def _hgrn_elementwise(q_ref, f_ref, v_ref, lb, sc):
    sc.q[...] = _silu(q_ref[...])
    f = lb + (1.0 - lb) * jax.nn.sigmoid(f_ref[...])
    sc.k[...] = 1.0 - f
    hi, mid, lo = _split3(jnp.log(f))
    sc.lg3[:, 0:D_HG] = hi
    sc.lg3[:, D_HG:2 * D_HG] = mid
    sc.lg3[:, 2 * D_HG:3 * D_HG] = lo
    sc.v16[...] = v_ref[...].astype(BF16)


def _hgrn_cumsum(sc, tri, n_chunks):
    for c in range(n_chunks):
        rows = _chunk_rows(c)
        p = _dot(tri, sc.lg3[rows, :])
        sc.b[rows, :] = p[:, 0:D_HG] + p[:, D_HG:2 * D_HG] + p[:, 2 * D_HG:3 * D_HG]


def _hgrn_factors(sc, n_chunks, *, reverse):
    first, last, ref = _chunk_marks(reverse)
    n_wide = None
    for c in range(n_chunks):
        rows = _chunk_rows(c)
        b2 = sc.b[rows, :] * LOG2_E
        r2 = b2[ref:ref + 1, :]
        end2 = b2[last:last + 1, :]
        spread = jnp.maximum(b2[first:first + 1, :] - r2, r2 - end2)
        wide = jnp.where(spread < HG_SAFE_EXPONENT_LOG2, 0.0, 1.0)
        sc.wide[c:c + 1, :] = wide
        n_wide = wide if n_wide is None else n_wide + wide
        narrow = jnp.broadcast_to(wide, b2.shape) == 0.0
        q, k = sc.q[rows, :], sc.k[rows, :]
        sc.qm[rows, :] = jnp.where(narrow, q * jnp.exp2(b2 - r2), 0.0).astype(BF16)
        sc.km[rows, :] = jnp.where(narrow, k * jnp.exp2(r2 - b2), 0.0).astype(BF16)
        sc.qi[rows, :] = (q * jnp.exp2(b2)).astype(BF16)
        sc.ks[rows, :] = (k * jnp.exp2(end2 - b2)).astype(BF16)
        sc.dec[c:c + 1, :] = jnp.exp2(end2)
    return n_wide


def _hgrn_state_increment(sc, c, h):
    rows, sl = _chunk_rows(c), _head_lanes(h)
    sc.ds[c, h] = _dot_tn(sc.v16[rows, sl], sc.ks[rows, sl])


def _hgrn_intra_factored(sc, o_ref, tri, n_chunks):
    causal = tri > 0
    pairs = [(c, h) for c in range(n_chunks) for h in range(HG_HEADS)]
    for c, h in pairs:
        _hgrn_state_increment(sc, c, h)
    scores = {}
    for c, h in pairs:
        rows, sl = _chunk_rows(c), _head_lanes(h)
        scores[c, h] = jnp.where(causal, _dot_nt(sc.qm[rows, sl], sc.km[rows, sl]), 0.0).astype(BF16)
    for c, h in pairs:
        rows, sl = _chunk_rows(c), _head_lanes(h)
        o_ref[rows, sl] = _dot(scores[c, h], sc.v16[rows, sl])


def _hgrn_intra_halving(sc, o_ref, n_chunks, *, reverse):
    cs = HG_CHUNK
    row = lax.broadcasted_iota(jnp.int32, (cs, cs), 0)
    col = lax.broadcasted_iota(jnp.int32, (cs, cs), 1)
    pos = lax.broadcasted_iota(jnp.int32, (cs, D_HG), 0)
    for c in range(n_chunks):
        rows = _chunk_rows(c)
        wide = jnp.broadcast_to(sc.wide[c:c + 1, :], (cs, D_HG)) > 0.0
        b, k = sc.b[rows, :], sc.k[rows, :]
        q = jnp.where(wide, sc.q[rows, :], 0.0)
        q16, k16 = q.astype(BF16), k.astype(BF16)
        scores = [jnp.where(row == col, _dot_nt(q16[:, _head_lanes(h)], k16[:, _head_lanes(h)]), 0.0)
                  for h in range(HG_HEADS)]
        b_hi, b_mid, b_lo = _split3(b)
        shift = 1
        while (1 << shift) <= cs:
            blk, half = 1 << shift, 1 << (shift - 1)
            boundary = ((row >> shift) << shift) + (half if reverse else half - 1)
            pick = jnp.where(col == boundary, 1.0, 0.0).astype(BF16)
            b_bnd = _dot(pick, b_hi) + _dot(pick, b_mid) + _dot(pick, b_lo)
            in_half = (pos & (blk - 1)) >= half
            attends = jnp.logical_not(in_half) if reverse else in_half
            qh = jnp.where(attends, q * jnp.exp(b - b_bnd), 0.0).astype(BF16)
            kh = jnp.where(attends, 0.0, k * jnp.exp(b_bnd - b)).astype(BF16)
            same_block = (row >> shift) == (col >> shift)
            for h in range(HG_HEADS):
                sl = _head_lanes(h)
                scores[h] = scores[h] + jnp.where(same_block, _dot_nt(qh[:, sl], kh[:, sl]), 0.0)
            shift += 1
        for h in range(HG_HEADS):
            sl = _head_lanes(h)
            o_ref[rows, sl] += _dot(scores[h].astype(BF16), sc.v16[rows, sl])


def _hgrn_carry_state(sc, o_ref, s_ref, n_chunks, *, reverse):
    order = range(n_chunks - 1, -1, -1) if reverse else range(n_chunks)
    for c in order:
        rows = _chunk_rows(c)
        for h in range(HG_HEADS):
            sl = _head_lanes(h)
            st = s_ref[h]
            o_ref[rows, sl] += _dot_nt(sc.qi[rows, sl], st.astype(BF16))
            s_ref[h] = st * sc.dec[c:c + 1, sl] + sc.ds[c, h]


def _hgrn_kernel(qf_ref, ff_ref, vf_ref, qb_ref, fb_ref, vb_ref, lbf_ref, lbb_ref, of_ref, ob_ref, s_ref, *scratch):
    tm = of_ref.shape[0]
    n_chunks = tm // HG_CHUNK
    sc_f = _HgrnScratch(*(r.at[0] for r in scratch))
    sc_b = _HgrnScratch(*(r.at[1] for r in scratch))

    @pl.when(pl.program_id(1) == 0)
    def _():
        s_ref[...] = jnp.zeros_like(s_ref)

    row = lax.broadcasted_iota(jnp.int32, (HG_CHUNK, HG_CHUNK), 0)
    col = lax.broadcasted_iota(jnp.int32, (HG_CHUNK, HG_CHUNK), 1)
    tri_f = jnp.where(col <= row, 1.0, 0.0).astype(BF16)
    tri_b = jnp.where(col >= row, 1.0, 0.0).astype(BF16)

    _hgrn_elementwise(qf_ref, ff_ref, vf_ref, _lower_bound(lbf_ref), sc_f)
    _hgrn_elementwise(qb_ref, fb_ref, vb_ref, _lower_bound(lbb_ref), sc_b)
    _hgrn_cumsum(sc_f, tri_f, n_chunks)
    _hgrn_cumsum(sc_b, tri_b, n_chunks)
    n_wide = _hgrn_factors(sc_f, n_chunks, reverse=False) + _hgrn_factors(sc_b, n_chunks, reverse=True)
    _hgrn_intra_factored(sc_f, of_ref, tri_f, n_chunks)
    _hgrn_intra_factored(sc_b, ob_ref, tri_b, n_chunks)
    _hgrn_carry_state(sc_f, of_ref, s_ref.at[0], n_chunks, reverse=False)
    _hgrn_carry_state(sc_b, ob_ref, s_ref.at[1], n_chunks, reverse=True)

    @pl.when(jnp.max(n_wide) > 0.0)
    def _():
        _hgrn_intra_halving(sc_f, of_ref, n_chunks, reverse=False)
        _hgrn_intra_halving(sc_b, ob_ref, n_chunks, reverse=True)


def _hgrn_scan(z3d, lb_fwd, lb_bwd):
    bsz, t, _ = z3d.shape
    tm = SCAN_TILE
    n_t = t // tm
    n_chunks = tm // HG_CHUNK

    def fwd(col):
        return pl.BlockSpec((None, tm, D_HG), lambda b, j: (b, j, col))

    def bwd(col):
        return pl.BlockSpec((None, tm, D_HG), lambda b, j: (b, n_t - 1 - j, col))

    lb_spec = pl.BlockSpec(lb_fwd.shape, lambda b, j: (0, 0))
    tile = lambda width, dtype: pltpu.VMEM((2, tm, width), dtype)
    scratch = _HgrnScratch(
        q=tile(D_HG, F32), k=tile(D_HG, F32), lg3=tile(3 * D_HG, BF16), b=tile(D_HG, F32), v16=tile(D_HG, BF16),
        qm=tile(D_HG, BF16), km=tile(D_HG, BF16), qi=tile(D_HG, BF16), ks=tile(D_HG, BF16),
        ds=pltpu.VMEM((2, n_chunks, HG_HEADS, HG_HEAD_DIM, HG_HEAD_DIM), F32),
        dec=pltpu.VMEM((2, n_chunks, D_HG), F32), wide=pltpu.VMEM((2, n_chunks, D_HG), F32))
    return pl.pallas_call(
        _hgrn_kernel,
        grid=(bsz, n_t),
        in_specs=[fwd(COL_Q), fwd(COL_F_FWD), fwd(COL_I), bwd(COL_Q), bwd(COL_F_BWD), bwd(COL_I),
                  lb_spec, lb_spec],
        out_specs=[pl.BlockSpec((None, tm, D_HG), lambda b, j: (b, j, 0)),
                   pl.BlockSpec((None, tm, D_HG), lambda b, j: (b, n_t - 1 - j, 0))],
        out_shape=[jax.ShapeDtypeStruct((bsz, t, D_HG), F32)] * 2,
        scratch_shapes=[pltpu.VMEM((2, HG_HEADS, HG_HEAD_DIM, HG_HEAD_DIM), F32), *scratch],
        compiler_params=pltpu.CompilerParams(
            dimension_semantics=("arbitrary", "arbitrary"), vmem_limit_bytes=VMEM_LIMIT),
        name="hgrn_scan",
    )(z3d, z3d, z3d, z3d, z3d, z3d, lb_fwd, lb_bwd)


def _softplus(x):
    return jnp.maximum(x, 0.0) + jnp.log1p(jnp.exp(-jnp.abs(x)))


def _lane_groups(width):
    return [slice(g * LANES, (g + 1) * LANES) for g in range(width // LANES)]


def _segment_pitch(seg_len):
    assert seg_len % (2 * SUBLANES) == 0
    return seg_len + SUBLANES


def _segment_rows(stage_ref, i, seg_len):
    rows = pl.ds(i, SUBLANES, stride=_segment_pitch(seg_len))
    return jnp.concatenate([stage_ref[g, rows, :] for g in range(stage_ref.shape[0])], axis=1)


def _stage_segments(stage_ref, x_ref, seg_len):
    pitch = _segment_pitch(seg_len)
    for g, lanes in enumerate(_lane_groups(x_ref.shape[1])):
        for s in range(SUBLANES):
            stage_ref[g, s * pitch:s * pitch + seg_len, :] = x_ref[s * seg_len:(s + 1) * seg_len, lanes]


def _unstage_segments(h_ref, stage_ref, seg_len):
    pitch = _segment_pitch(seg_len)
    for g, lanes in enumerate(_lane_groups(h_ref.shape[1])):
        for s in range(SUBLANES):
            h_ref[s * seg_len:(s + 1) * seg_len, lanes] = stage_ref[g, s * pitch:s * pitch + seg_len, :]


def _shift_segments(v, fill_row, *, down):
    sub = lax.broadcasted_iota(jnp.int32, v.shape, 0)
    if down:
        return jnp.where(sub >= 1, pltpu.roll(v, 1, 0), fill_row)
    return jnp.where(sub < SUBLANES - 1, pltpu.roll(v, SUBLANES - 1, 0), fill_row)


def _segmented_scan(a, u, carry, *, reverse):
    n = len(a)
    order = range(n - 1, -1, -1) if reverse else range(n)
    h, a_cum, last = [None] * n, [None] * n, None
    for i in order:
        if last is None:
            h[i], a_cum[i] = u[i], a[i]
        else:
            h[i], a_cum[i] = a[i] * h[last] + u[i], a[i] * a_cum[last]
        last = i
    h_end, a_end = h[last], a_cum[last]
    entering = [None] * SUBLANES
    for s in (range(SUBLANES - 1, -1, -1) if reverse else range(SUBLANES)):
        entering[s] = carry
        carry = h_end[s:s + 1, :] + a_end[s:s + 1, :] * carry
    entering = jnp.concatenate(entering, axis=0)
    return [h[i] + a_cum[i] * entering for i in range(n)], carry


def _rglru_direction(x_ref, prev_ref, next_ref, cw_ref, cb_ref, wa_ref, ba_ref, wx_ref, bx_ref,
                     lam_ref, h_ref, stage_ref, carry_ref, first_tile, last_tile, *, reverse):
    tm = x_ref.shape[0]
    n = tm // SUBLANES
    _stage_segments(stage_ref, x_ref, n)
    xs = [_segment_rows(stage_ref, i, n) for i in range(n)]
    prev = jnp.where(first_tile, 0.0, prev_ref[...])
    nxt = jnp.where(last_tile, 0.0, next_ref[...])
    edge = {-2: _shift_segments(xs[n - 2], prev[HALO - 2:HALO - 1, :], down=True),
            -1: _shift_segments(xs[n - 1], prev[HALO - 1:HALO, :], down=True),
            n: _shift_segments(xs[0], nxt[0:1, :], down=False)}
    tap = lambda i: xs[i] if 0 <= i < n else edge[i]
    cw = [cw_ref[w:w + 1, :] for w in range(RG_CONV_W)]
    cb = cb_ref[...]
    xc = []
    for i in range(n):
        acc = cb + tap(i - RG_CONV_LEFT) * cw[0]
        for w in range(1, RG_CONV_W):
            acc = acc + tap(i - RG_CONV_LEFT + w) * cw[w]
        xc.append(acc)
    xc = jnp.concatenate(xc, axis=0)
    xc16 = xc.astype(BF16)
    r = jax.nn.sigmoid(_dot(xc16, wa_ref[...]) + ba_ref[...])
    gate_i = jax.nn.sigmoid(_dot(xc16, wx_ref[...]) + bx_ref[...])
    log_a = (-RG_C) * r * _softplus(-lam_ref[...])
    a = jnp.exp(log_a)
    y = -jnp.tanh(log_a) * (a * a + 1.0)
    u = jnp.where(y > 0.0, y * lax.rsqrt(y), 0.0) * (gate_i * xc)
    vec = lambda v, i: v[i * SUBLANES:(i + 1) * SUBLANES, :]
    h, carry = _segmented_scan([vec(a, i) for i in range(n)], [vec(u, i) for i in range(n)], carry_ref[...],
                               reverse=reverse)
    carry_ref[...] = carry
    for i in range(n):
        for g, lanes in enumerate(_lane_groups(D_RG)):
            stage_ref[g, pl.ds(i, SUBLANES, stride=_segment_pitch(n)), :] = h[i][:, lanes]
    _unstage_segments(h_ref, stage_ref, n)


def _rglru_fwd_kernel(x_ref, prev_ref, next_ref, cw_ref, cb_ref, wa_ref, ba_ref, wx_ref, bx_ref, lam_ref,
                      h_ref, stage_ref, carry_ref):
    j = pl.program_id(1)

    @pl.when(j == 0)
    def _():
        carry_ref[...] = jnp.zeros_like(carry_ref)

    _rglru_direction(x_ref, prev_ref, next_ref, cw_ref, cb_ref, wa_ref, ba_ref, wx_ref, bx_ref, lam_ref,
                     h_ref, stage_ref, carry_ref, j == 0, j == pl.num_programs(1) - 1, reverse=False)


def _rx_tile_specs(tm, t, tile_of):
    halo_per_tile = tm // HALO
    n_halo = t // HALO
    main = pl.BlockSpec((None, tm, D_RG), lambda b, j: (b, tile_of(j), COL_RX))
    prev = pl.BlockSpec(
        (None, HALO, D_RG), lambda b, j: (b, jnp.maximum(tile_of(j) * halo_per_tile - 1, 0), COL_RX))
    nxt = pl.BlockSpec(
        (None, HALO, D_RG), lambda b, j: (b, jnp.minimum((tile_of(j) + 1) * halo_per_tile, n_halo - 1), COL_RX))
    return [main, prev, nxt]


def _rglru_scratch(tm):
    return [pltpu.VMEM((D_RG // LANES, SUBLANES * _segment_pitch(tm // SUBLANES), LANES), F32),
            pltpu.VMEM((1, D_RG), F32)]


def _whole(a):
    return pl.BlockSpec(a.shape, lambda *_: (0,) * a.ndim)


def _rglru_fwd_scan(z3d, conv_w, conv_b, gates):
    bsz, t, _ = z3d.shape
    tm = SCAN_TILE
    params = [conv_w, conv_b, *gates]
    return pl.pallas_call(
        _rglru_fwd_kernel,
        grid=(bsz, t // tm),
        in_specs=_rx_tile_specs(tm, t, lambda j: j) + [_whole(p) for p in params],
        out_specs=pl.BlockSpec((None, tm, D_RG), lambda b, j: (b, j, 0)),
        out_shape=jax.ShapeDtypeStruct((bsz, t, D_RG), F32),
        scratch_shapes=_rglru_scratch(tm),
        compiler_params=pltpu.CompilerParams(
            dimension_semantics=("arbitrary", "arbitrary"), vmem_limit_bytes=VMEM_LIMIT),
        name="rglru_fwd_scan",
    )(z3d, z3d, z3d, *params)


def _gelu_tanh(x):
    return 0.5 * x * (1.0 + jnp.tanh(0.7978845608028654 * (x + 0.044715 * (x * x * x))))


def _mix_residual(x, gate, of_ref, ob_ref, g_ref, hf_ref, hb_ref, rgate_ref, hgain_ref, w_ref):
    o = of_ref[...] + ob_ref[...]
    g = _silu(g_ref[...])
    o_rg = ((hf_ref[...] + hb_ref[...]) * _gelu_tanh(rgate_ref[...])).astype(BF16)
    y = _dot(o_rg, w_ref[D_HG:, :])
    for h in range(HG_HEADS):
        sl = _head_lanes(h)
        o_hg = (_rms(o[:, sl], hgain_ref[...]) * g[:, sl]).astype(BF16)
        y = y + _dot(o_hg, w_ref[sl, :])
    return x + gate * y


def _mixout_kernel(x_ref, mod_ref, of_ref, ob_ref, g_ref, hf_ref, rgate_ref, rx_ref, rxprev_ref, rxnext_ref,
                   cw_ref, cb_ref, wa_ref, ba_ref, wx_ref, bx_ref, lam_ref, hgain_ref, wout_ref,
                   gain_ref, w1_ref, w3_ref, w2_ref, fgain_ref, o_ref, hb_ref, stage_ref, carry_ref):
    s = pl.program_id(1)
    n_t = pl.num_programs(1) - 1
    scan_tile = n_t - 1 - s

    def scan():
        _rglru_direction(rx_ref, rxprev_ref, rxnext_ref, cw_ref, cb_ref, wa_ref, ba_ref, wx_ref, bx_ref, lam_ref,
                         hb_ref, stage_ref, carry_ref, scan_tile == 0, scan_tile == n_t - 1,
                         reverse=True)

    def mix_ffn():
        x2 = _mix_residual(x_ref[...], mod_ref[5:6, :], of_ref, ob_ref, g_ref, hf_ref, hb_ref,
                           rgate_ref, hgain_ref, wout_ref)
        x3 = _swiglu_residual(x2, mod_ref, 6, gain_ref, w1_ref, w3_ref, w2_ref)
        o_ref[...] = _rms(x3, fgain_ref[...])

    @pl.when(s == 0)
    def _():
        carry_ref[...] = jnp.zeros_like(carry_ref)
        scan()

    @pl.when(jnp.logical_and(s > 0, s < n_t))
    def _():
        mix_ffn()
        scan()

    @pl.when(s == n_t)
    def _():
        mix_ffn()


def _mixout(x3d, mod, o_f, o_b, z3d, h_f, conv_w, conv_b, gates_bwd, hg_gain, w_out, gain, w1, w3, w2, final_gain):
    bsz, t, d = x3d.shape
    tm = ROW_TILE
    n_t = t // tm
    ffn_tile = lambda s: jnp.minimum(n_t - s, n_t - 1)
    scan_tile = lambda s: jnp.maximum(n_t - 1 - s, 0)

    def rows(width, col=0):
        return pl.BlockSpec((None, tm, width), lambda b, s: (b, ffn_tile(s), col))

    scan_params = [conv_w, conv_b, *gates_bwd]
    ffn_params = [hg_gain, w_out, gain, w1, w3, w2, final_gain]
    return pl.pallas_call(
        _mixout_kernel,
        grid=(bsz, n_t + 1),
        in_specs=[rows(d), pl.BlockSpec((None, N_MOD, d), lambda b, s: (b, 0, 0)),
                  rows(D_HG), rows(D_HG), rows(D_HG, COL_G), rows(D_RG), rows(D_RG, COL_RGATE),
                  *_rx_tile_specs(tm, t, scan_tile),
                  *[_whole(p) for p in scan_params], *[_resident(p.shape) for p in ffn_params]],
        out_specs=rows(d),
        out_shape=jax.ShapeDtypeStruct((bsz, t, d), F32),
        scratch_shapes=[pltpu.VMEM((tm, D_RG), F32), *_rglru_scratch(tm)],
        compiler_params=pltpu.CompilerParams(
            dimension_semantics=("arbitrary", "arbitrary"), vmem_limit_bytes=VMEM_LIMIT),
        name="mixout",
    )(x3d, mod, o_f, o_b, z3d, h_f, z3d, z3d, z3d, z3d, *scan_params, *ffn_params)


def _block_diag(w):
    nb, d, e = w.shape
    eye = jnp.eye(nb, dtype=w.dtype)
    return (eye[:, None, :, None] * w[:, :, None, :]).reshape(nb * d, nb * e)


def kernel(x, c, w_ada, b_ada, ffn1_norm, ffn1_w1, ffn1_w3, ffn1_w2, mix_norm, w_in, hg_lb_fwd, hg_lb_bwd, hg_out_norm, rg_conv_w, rg_conv_b, rg_wa_fwd, rg_ba_fwd, rg_wx_fwd, rg_bx_fwd, rg_lam_fwd, rg_wa_bwd, rg_ba_bwd, rg_wx_bwd, rg_bx_bwd, rg_lam_bwd, w_out, ffn2_norm, ffn2_w1, ffn2_w3, ffn2_w2, final_norm):
    bsz, t, d = x.shape
    depth = w_ada.shape[0]
    assert depth == 1 and hg_lb_fwd.shape[0] == 2, "kernels are written for a single layer"
    assert t % ROW_TILE == 0 and SCAN_TILE == ROW_TILE and SCAN_TILE % HG_CHUNK == 0
    tiles_per_batch = t // ROW_TILE
    row = lambda v: v.reshape(1, -1)
    l = 0

    mod = _modulation(c, w_ada[l], b_ada[l])
    bf16 = lambda w: w[l].astype(BF16)
    flat = lambda a: a.reshape(bsz * t, a.shape[-1])

    x1, z2d = _ffn_inproj(flat(x), mod, row(ffn1_norm[l]), bf16(ffn1_w1), bf16(ffn1_w3), bf16(ffn1_w2),
                          row(mix_norm[l]), bf16(w_in), tiles_per_batch=tiles_per_batch)
    z3d = z2d.reshape(bsz, t, IN_COLS)
    o_f, o_b = _hgrn_scan(z3d, hg_lb_fwd, hg_lb_bwd)

    def gates(wa, ba, wx, bx, lam):
        return (_block_diag(wa[l]).astype(BF16), row(ba[l]), _block_diag(wx[l]).astype(BF16), row(bx[l]),
                row(lam[l]))

    conv_w, conv_b = rg_conv_w[l], row(rg_conv_b[l])
    h_f = _rglru_fwd_scan(z3d, conv_w, conv_b, gates(rg_wa_fwd, rg_ba_fwd, rg_wx_fwd, rg_bx_fwd, rg_lam_fwd))

    return _mixout(x1.reshape(bsz, t, d), mod, o_f, o_b, z3d, h_f, conv_w, conv_b,
                   gates(rg_wa_bwd, rg_ba_bwd, rg_wx_bwd, rg_bx_bwd, rg_lam_bwd), row(hg_out_norm[l]), bf16(w_out),
                   row(ffn2_norm[l]), bf16(ffn2_w1), bf16(ffn2_w3), bf16(ffn2_w2), row(final_norm))
```

```python
import math
from typing import NamedTuple

import jax
import jax.numpy as jnp
from jax import lax
from jax.experimental import pallas as pl
from jax.experimental.pallas import tpu as pltpu

F32 = jnp.float32
BF16 = jnp.bfloat16
Z_DTYPE = BF16

EPS = 1e-6
N_MOD = 9
HG_HEAD_DIM = 128
HG_HEADS = 4
D_HG = HG_HEADS * HG_HEAD_DIM
RG_BLOCKS = 8
D_RG = 512
RG_CONV_W = 4
RG_CONV_LEFT = RG_CONV_W // 2
RG_C = 8.0
IN_COLS = 5 * D_HG + 2 * D_RG
COL_Q, COL_F_FWD, COL_F_BWD, COL_I, COL_G, COL_RX, COL_RGATE = range(7)

SUBLANES = 8
LANES = 128
HALO = 2 * SUBLANES
HG_CHUNK = 64
LOG2_E = math.log2(math.e)
HG_SAFE_EXPONENT_LOG2 = 115.0
VMEM_LIMIT = 56 * 1024 * 1024

ROW_TILE = 256
SCAN_TILE = 256


def _rms(x, gain):
    return x * lax.rsqrt(jnp.mean(x * x, axis=-1, keepdims=True) + EPS) * gain


def _norm_mod(x, gain, shift, scale):
    return _rms(x, gain) * (1.0 + scale) + shift


def _silu(x):
    return x * jax.nn.sigmoid(x)


def _dot(a, b):
    return jnp.dot(a, b, preferred_element_type=F32)


def _dot_nt(a, b):
    return lax.dot_general(a, b, (((1,), (1,)), ((), ())), preferred_element_type=F32)


def _dot_tn(a, b):
    return lax.dot_general(a, b, (((0,), (0,)), ((), ())), preferred_element_type=F32)


def _mod_kernel(c_ref, w_ref, b_ref, o_ref):
    o_ref[...] = _dot(_silu(c_ref[...]), w_ref[...]) + b_ref[...]


def _modulation(c, w_ada, b_ada):
    bsz, d = c.shape
    n = w_ada.shape[1]
    bn = 1024
    c_pad = jnp.pad(c, ((0, SUBLANES - bsz), (0, 0)))
    out = pl.pallas_call(
        _mod_kernel,
        grid=(n // bn,),
        in_specs=[
            pl.BlockSpec((SUBLANES, d), lambda j: (0, 0)),
            pl.BlockSpec((d, bn), lambda j: (0, j)),
            pl.BlockSpec((1, bn), lambda j: (0, j)),
        ],
        out_specs=pl.BlockSpec((SUBLANES, bn), lambda j: (0, j)),
        out_shape=jax.ShapeDtypeStruct((SUBLANES, n), F32),
        name="modulation",
    )(c_pad, w_ada, b_ada.reshape(1, n))
    return out[:bsz].reshape(bsz, N_MOD, d)


def _swiglu_residual(x, mod_ref, mod_base, gain_ref, w1_ref, w3_ref, w2_ref):
    shift = mod_ref[mod_base:mod_base + 1, :]
    scale = mod_ref[mod_base + 1:mod_base + 2, :]
    gate = mod_ref[mod_base + 2:mod_base + 3, :]
    h = _norm_mod(x, gain_ref[...], shift, scale).astype(BF16)
    a = _dot(h, w1_ref[...])
    b = _dot(h, w3_ref[...])
    g = (_silu(a) * b).astype(BF16)
    return x + (0.5 * gate) * _dot(g, w2_ref[...])


def _ffn_inproj_kernel(x_ref, mod_ref, gain_ref, w1_ref, w3_ref, w2_ref, mix_gain_ref, win_ref, x1_ref, z_ref):
    x1 = _swiglu_residual(x_ref[...], mod_ref, 0, gain_ref, w1_ref, w3_ref, w2_ref)
    x1_ref[...] = x1
    h = _norm_mod(x1, mix_gain_ref[...], mod_ref[3:4, :], mod_ref[4:5, :]).astype(BF16)
    z_ref[...] = _dot(h, win_ref[...]).astype(z_ref.dtype)


def _resident(shape):
    return pl.BlockSpec(shape, lambda *_: (0,) * len(shape), pipeline_mode=pl.Buffered(1))


def _row_tiles(tm, width, col=0):
    return pl.BlockSpec((tm, width), lambda i: (i, col))


def _mod_rows(d, tiles_per_batch):
    return pl.BlockSpec((None, N_MOD, d), lambda i: (i // tiles_per_batch, 0, 0))


def _ffn_inproj(x2d, mod, gain, w1, w3, w2, mix_gain, w_in, *, tiles_per_batch):
    n, d = x2d.shape
    cols = w_in.shape[1]
    tm = ROW_TILE
    return pl.pallas_call(
        _ffn_inproj_kernel,
        grid=(n // tm,),
        in_specs=[_row_tiles(tm, d), _mod_rows(d, tiles_per_batch), _resident(gain.shape), _resident(w1.shape),
                  _resident(w3.shape), _resident(w2.shape), _resident(mix_gain.shape), _resident(w_in.shape)],
        out_specs=[_row_tiles(tm, d), _row_tiles(tm, cols)],
        out_shape=[jax.ShapeDtypeStruct((n, d), F32), jax.ShapeDtypeStruct((n, cols), Z_DTYPE)],
        compiler_params=pltpu.CompilerParams(
            dimension_semantics=("arbitrary",), vmem_limit_bytes=VMEM_LIMIT),
        name="ffn_inproj",
    )(x2d, mod, gain, w1, w3, w2, mix_gain, w_in)


def _lower_bound(lb_ref):
    p0 = lb_ref[0:1, :]
    p1 = lb_ref[1:2, :]
    m = jnp.maximum(p0, p1)
    e0 = jnp.exp(p0 - m)
    e1 = jnp.exp(p1 - m)
    return e1 / (e0 + e1)


def _split3(x):
    hi = x.astype(BF16)
    r = x - hi.astype(F32)
    mid = r.astype(BF16)
    lo = (r - mid.astype(F32)).astype(BF16)
    return hi, mid, lo


class _HgrnScratch(NamedTuple):
    q: object
    k: object
    lg3: object
    b: object
    v16: object
    qm: object
    km: object
    qi: object
    ks: object
    ds: object
    dec: object
    wide: object


def _chunk_rows(c):
    return slice(c * HG_CHUNK, (c + 1) * HG_CHUNK)


def _head_lanes(h):
    return slice(h * HG_HEAD_DIM, (h + 1) * HG_HEAD_DIM)


def _chunk_marks(reverse):
    c = HG_CHUNK
    return (c - 1, 0, c // 2) if reverse else (0, c - 1, c // 2 - 1)


def _hgrn_elementwise(q_ref, f_ref, v_ref, lb, sc):
    sc.q[...] = _silu(q_ref[...].astype(F32))
    f = lb + (1.0 - lb) * jax.nn.sigmoid(f_ref[...].astype(F32))
    sc.k[...] = 1.0 - f
    hi, mid, lo = _split3(jnp.log(f))
    sc.lg3[:, 0:D_HG] = hi
    sc.lg3[:, D_HG:2 * D_HG] = mid
    sc.lg3[:, 2 * D_HG:3 * D_HG] = lo
    sc.v16[...] = v_ref[...].astype(BF16)


def _hgrn_cumsum(sc, tri, n_chunks):
    for c in range(n_chunks):
        rows = _chunk_rows(c)
        p = _dot(tri, sc.lg3[rows, :])
        sc.b[rows, :] = p[:, 0:D_HG] + p[:, D_HG:2 * D_HG] + p[:, 2 * D_HG:3 * D_HG]


def _hgrn_factors(sc, n_chunks, *, reverse):
    first, last, ref = _chunk_marks(reverse)
    n_wide = None
    for c in range(n_chunks):
        rows = _chunk_rows(c)
        b2 = sc.b[rows, :] * LOG2_E
        r2 = b2[ref:ref + 1, :]
        end2 = b2[last:last + 1, :]
        spread = jnp.maximum(b2[first:first + 1, :] - r2, r2 - end2)
        wide = jnp.where(spread < HG_SAFE_EXPONENT_LOG2, 0.0, 1.0)
        sc.wide[c:c + 1, :] = wide
        n_wide = wide if n_wide is None else n_wide + wide
        narrow = jnp.broadcast_to(wide, b2.shape) == 0.0
        q, k = sc.q[rows, :], sc.k[rows, :]
        sc.qm[rows, :] = jnp.where(narrow, q * jnp.exp2(b2 - r2), 0.0).astype(BF16)
        sc.km[rows, :] = jnp.where(narrow, k * jnp.exp2(r2 - b2), 0.0).astype(BF16)
        sc.qi[rows, :] = (q * jnp.exp2(b2)).astype(BF16)
        sc.ks[rows, :] = (k * jnp.exp2(end2 - b2)).astype(BF16)
        sc.dec[c:c + 1, :] = jnp.exp2(end2)
    return n_wide


def _hgrn_state_increment(sc, c, h):
    rows, sl = _chunk_rows(c), _head_lanes(h)
    sc.ds[c, h] = _dot_tn(sc.v16[rows, sl], sc.ks[rows, sl])


def _hgrn_intra_factored(sc, o_ref, tri, n_chunks):
    causal = tri > 0
    pairs = [(c, h) for c in range(n_chunks) for h in range(HG_HEADS)]
    for c, h in pairs:
        _hgrn_state_increment(sc, c, h)
    scores = {}
    for c, h in pairs:
        rows, sl = _chunk_rows(c), _head_lanes(h)
        scores[c, h] = jnp.where(causal, _dot_nt(sc.qm[rows, sl], sc.km[rows, sl]), 0.0).astype(BF16)
    for c, h in pairs:
        rows, sl = _chunk_rows(c), _head_lanes(h)
        o_ref[rows, sl] = _dot(scores[c, h], sc.v16[rows, sl])


def _hgrn_intra_halving(sc, o_ref, n_chunks, *, reverse):
    cs = HG_CHUNK
    row = lax.broadcasted_iota(jnp.int32, (cs, cs), 0)
    col = lax.broadcasted_iota(jnp.int32, (cs, cs), 1)
    pos = lax.broadcasted_iota(jnp.int32, (cs, D_HG), 0)
    for c in range(n_chunks):
        rows = _chunk_rows(c)
        wide = jnp.broadcast_to(sc.wide[c:c + 1, :], (cs, D_HG)) > 0.0
        b, k = sc.b[rows, :], sc.k[rows, :]
        q = jnp.where(wide, sc.q[rows, :], 0.0)
        q16, k16 = q.astype(BF16), k.astype(BF16)
        scores = [jnp.where(row == col, _dot_nt(q16[:, _head_lanes(h)], k16[:, _head_lanes(h)]), 0.0)
                  for h in range(HG_HEADS)]
        b_hi, b_mid, b_lo = _split3(b)
        shift = 1
        while (1 << shift) <= cs:
            blk, half = 1 << shift, 1 << (shift - 1)
            boundary = ((row >> shift) << shift) + (half if reverse else half - 1)
            pick = jnp.where(col == boundary, 1.0, 0.0).astype(BF16)
            b_bnd = _dot(pick, b_hi) + _dot(pick, b_mid) + _dot(pick, b_lo)
            in_half = (pos & (blk - 1)) >= half
            attends = jnp.logical_not(in_half) if reverse else in_half
            qh = jnp.where(attends, q * jnp.exp(b - b_bnd), 0.0).astype(BF16)
            kh = jnp.where(attends, 0.0, k * jnp.exp(b_bnd - b)).astype(BF16)
            same_block = (row >> shift) == (col >> shift)
            for h in range(HG_HEADS):
                sl = _head_lanes(h)
                scores[h] = scores[h] + jnp.where(same_block, _dot_nt(qh[:, sl], kh[:, sl]), 0.0)
            shift += 1
        for h in range(HG_HEADS):
            sl = _head_lanes(h)
            o_ref[rows, sl] += _dot(scores[h].astype(BF16), sc.v16[rows, sl])


def _hgrn_carry_state(sc, o_ref, s_ref, n_chunks, *, reverse):
    order = range(n_chunks - 1, -1, -1) if reverse else range(n_chunks)
    for c in order:
        rows = _chunk_rows(c)
        for h in range(HG_HEADS):
            sl = _head_lanes(h)
            st = s_ref[h]
            o_ref[rows, sl] += _dot_nt(sc.qi[rows, sl], st.astype(BF16))
            s_ref[h] = st * sc.dec[c:c + 1, sl] + sc.ds[c, h]


def _hgrn_kernel(qf_ref, ff_ref, vf_ref, qb_ref, fb_ref, vb_ref, lbf_ref, lbb_ref, of_ref, ob_ref, s_ref, *scratch):
    tm = of_ref.shape[0]
    n_chunks = tm // HG_CHUNK
    sc_f = _HgrnScratch(*(r.at[0] for r in scratch))
    sc_b = _HgrnScratch(*(r.at[1] for r in scratch))

    @pl.when(pl.program_id(1) == 0)
    def _():
        s_ref[...] = jnp.zeros_like(s_ref)

    row = lax.broadcasted_iota(jnp.int32, (HG_CHUNK, HG_CHUNK), 0)
    col = lax.broadcasted_iota(jnp.int32, (HG_CHUNK, HG_CHUNK), 1)
    tri_f = jnp.where(col <= row, 1.0, 0.0).astype(BF16)
    tri_b = jnp.where(col >= row, 1.0, 0.0).astype(BF16)

    _hgrn_elementwise(qf_ref, ff_ref, vf_ref, _lower_bound(lbf_ref), sc_f)
    _hgrn_elementwise(qb_ref, fb_ref, vb_ref, _lower_bound(lbb_ref), sc_b)
    _hgrn_cumsum(sc_f, tri_f, n_chunks)
    _hgrn_cumsum(sc_b, tri_b, n_chunks)
    n_wide = _hgrn_factors(sc_f, n_chunks, reverse=False) + _hgrn_factors(sc_b, n_chunks, reverse=True)
    _hgrn_intra_factored(sc_f, of_ref, tri_f, n_chunks)
    _hgrn_intra_factored(sc_b, ob_ref, tri_b, n_chunks)
    _hgrn_carry_state(sc_f, of_ref, s_ref.at[0], n_chunks, reverse=False)
    _hgrn_carry_state(sc_b, ob_ref, s_ref.at[1], n_chunks, reverse=True)

    @pl.when(jnp.max(n_wide) > 0.0)
    def _():
        _hgrn_intra_halving(sc_f, of_ref, n_chunks, reverse=False)
        _hgrn_intra_halving(sc_b, ob_ref, n_chunks, reverse=True)


def _hgrn_scan(z3d, lb_fwd, lb_bwd):
    bsz, t, _ = z3d.shape
    tm = SCAN_TILE
    n_t = t // tm
    n_chunks = tm // HG_CHUNK

    def fwd(col):
        return pl.BlockSpec((None, tm, D_HG), lambda b, j: (b, j, col))

    def bwd(col):
        return pl.BlockSpec((None, tm, D_HG), lambda b, j: (b, n_t - 1 - j, col))

    lb_spec = pl.BlockSpec(lb_fwd.shape, lambda b, j: (0, 0))
    tile = lambda width, dtype: pltpu.VMEM((2, tm, width), dtype)
    scratch = _HgrnScratch(
        q=tile(D_HG, F32), k=tile(D_HG, F32), lg3=tile(3 * D_HG, BF16), b=tile(D_HG, F32), v16=tile(D_HG, BF16),
        qm=tile(D_HG, BF16), km=tile(D_HG, BF16), qi=tile(D_HG, BF16), ks=tile(D_HG, BF16),
        ds=pltpu.VMEM((2, n_chunks, HG_HEADS, HG_HEAD_DIM, HG_HEAD_DIM), F32),
        dec=pltpu.VMEM((2, n_chunks, D_HG), F32), wide=pltpu.VMEM((2, n_chunks, D_HG), F32))
    return pl.pallas_call(
        _hgrn_kernel,
        grid=(bsz, n_t),
        in_specs=[fwd(COL_Q), fwd(COL_F_FWD), fwd(COL_I), bwd(COL_Q), bwd(COL_F_BWD), bwd(COL_I),
                  lb_spec, lb_spec],
        out_specs=[pl.BlockSpec((None, tm, D_HG), lambda b, j: (b, j, 0)),
                   pl.BlockSpec((None, tm, D_HG), lambda b, j: (b, n_t - 1 - j, 0))],
        out_shape=[jax.ShapeDtypeStruct((bsz, t, D_HG), F32)] * 2,
        scratch_shapes=[pltpu.VMEM((2, HG_HEADS, HG_HEAD_DIM, HG_HEAD_DIM), F32), *scratch],
        compiler_params=pltpu.CompilerParams(
            dimension_semantics=("arbitrary", "arbitrary"), vmem_limit_bytes=VMEM_LIMIT),
        name="hgrn_scan",
    )(z3d, z3d, z3d, z3d, z3d, z3d, lb_fwd, lb_bwd)


def _softplus(x):
    return jnp.maximum(x, 0.0) + jnp.log1p(jnp.exp(-jnp.abs(x)))


def _lane_groups(width):
    return [slice(g * LANES, (g + 1) * LANES) for g in range(width // LANES)]


def _segment_pitch(seg_len):
    assert seg_len % (2 * SUBLANES) == 0
    return seg_len + SUBLANES


def _segment_rows(stage_ref, i, seg_len):
    rows = pl.ds(i, SUBLANES, stride=_segment_pitch(seg_len))
    return jnp.concatenate([stage_ref[g, rows, :] for g in range(stage_ref.shape[0])], axis=1)


def _stage_segments(stage_ref, x_ref, seg_len):
    pitch = _segment_pitch(seg_len)
    for g, lanes in enumerate(_lane_groups(x_ref.shape[1])):
        for s in range(SUBLANES):
            stage_ref[g, s * pitch:s * pitch + seg_len, :] = x_ref[s * seg_len:(s + 1) * seg_len, lanes].astype(F32)


def _unstage_segments(h_ref, stage_ref, seg_len):
    pitch = _segment_pitch(seg_len)
    for g, lanes in enumerate(_lane_groups(h_ref.shape[1])):
        for s in range(SUBLANES):
            h_ref[s * seg_len:(s + 1) * seg_len, lanes] = stage_ref[g, s * pitch:s * pitch + seg_len, :]


def _shift_segments(v, fill_row, *, down):
    sub = lax.broadcasted_iota(jnp.int32, v.shape, 0)
    if down:
        return jnp.where(sub >= 1, pltpu.roll(v, 1, 0), fill_row)
    return jnp.where(sub < SUBLANES - 1, pltpu.roll(v, SUBLANES - 1, 0), fill_row)


class _SegmentedScan:
    def __init__(self, n, reverse):
        self.n, self.reverse = n, reverse
        self.h, self.a_cum, self.last = [None] * n, [None] * n, None

    def order(self):
        return list(range(self.n - 1, -1, -1) if self.reverse else range(self.n))

    def feed(self, i, a, u):
        if self.last is None:
            self.h[i], self.a_cum[i] = u, a
        else:
            self.h[i], self.a_cum[i] = a * self.h[self.last] + u, a * self.a_cum[self.last]
        self.last = i

    def finish(self, carry):
        h_end, a_end = self.h[self.last], self.a_cum[self.last]
        entering = [None] * SUBLANES
        for s in (range(SUBLANES - 1, -1, -1) if self.reverse else range(SUBLANES)):
            entering[s] = carry
            carry = h_end[s:s + 1, :] + a_end[s:s + 1, :] * carry
        entering = jnp.concatenate(entering, axis=0)
        return [self.h[i] + self.a_cum[i] * entering for i in range(self.n)], carry


def _rglru_conv_gates(x_ref, prev_ref, next_ref, cw_ref, cb_ref, wa_ref, wx_ref, stage_ref, first_tile, last_tile):
    tm = x_ref.shape[0]
    n = tm // SUBLANES
    _stage_segments(stage_ref, x_ref, n)
    xs = [_segment_rows(stage_ref, i, n) for i in range(n)]
    prev = jnp.where(first_tile, 0.0, prev_ref[...].astype(F32))
    nxt = jnp.where(last_tile, 0.0, next_ref[...].astype(F32))
    edge = {-2: _shift_segments(xs[n - 2], prev[HALO - 2:HALO - 1, :], down=True),
            -1: _shift_segments(xs[n - 1], prev[HALO - 1:HALO, :], down=True),
            n: _shift_segments(xs[0], nxt[0:1, :], down=False)}
    tap = lambda i: xs[i] if 0 <= i < n else edge[i]
    cw = [cw_ref[w:w + 1, :] for w in range(RG_CONV_W)]
    cb = cb_ref[...]
    xc = []
    for i in range(n):
        acc = cb + tap(i - RG_CONV_LEFT) * cw[0]
        for w in range(1, RG_CONV_W):
            acc = acc + tap(i - RG_CONV_LEFT + w) * cw[w]
        xc.append(acc)
    xc = jnp.concatenate(xc, axis=0)
    xc16 = xc.astype(BF16)
    return xc, _dot(xc16, wa_ref[...]), _dot(xc16, wx_ref[...])


def _rglru_recurrence(xc, pre_r, pre_i, ba_ref, bx_ref, lam_ref, h_ref, stage_ref, carry_ref, *, reverse):
    n = xc.shape[0] // SUBLANES
    r = jax.nn.sigmoid(pre_r + ba_ref[...])
    gate_i = jax.nn.sigmoid(pre_i + bx_ref[...])
    log_a = (-RG_C) * r * _softplus(-lam_ref[...])
    a = jnp.exp(log_a)
    y = -jnp.tanh(log_a) * (a * a + 1.0)
    u = jnp.where(y > 0.0, y * lax.rsqrt(y), 0.0) * (gate_i * xc)
    scan = _SegmentedScan(n, reverse)
    for i in scan.order():
        vec = slice(i * SUBLANES, (i + 1) * SUBLANES)
        scan.feed(i, a[vec, :], u[vec, :])
    h, carry = scan.finish(carry_ref[...])
    carry_ref[...] = carry
    for i in range(n):
        for g, lanes in enumerate(_lane_groups(D_RG)):
            stage_ref[g, pl.ds(i, SUBLANES, stride=_segment_pitch(n)), :] = h[i][:, lanes]
    _unstage_segments(h_ref, stage_ref, n)


def _rglru_direction(x_ref, prev_ref, next_ref, cw_ref, cb_ref, wa_ref, ba_ref, wx_ref, bx_ref,
                     lam_ref, h_ref, stage_ref, carry_ref, first_tile, last_tile, *, reverse):
    pre = _rglru_conv_gates(x_ref, prev_ref, next_ref, cw_ref, cb_ref, wa_ref, wx_ref, stage_ref,
                            first_tile, last_tile)
    _rglru_recurrence(*pre, ba_ref, bx_ref, lam_ref, h_ref, stage_ref, carry_ref, reverse=reverse)


def _rglru_fwd_kernel(x_ref, prev_ref, next_ref, cw_ref, cb_ref, wa_ref, ba_ref, wx_ref, bx_ref, lam_ref,
                      h_ref, stage_ref, carry_ref):
    j = pl.program_id(1)

    @pl.when(j == 0)
    def _():
        carry_ref[...] = jnp.zeros_like(carry_ref)

    _rglru_direction(x_ref, prev_ref, next_ref, cw_ref, cb_ref, wa_ref, ba_ref, wx_ref, bx_ref, lam_ref,
                     h_ref, stage_ref, carry_ref, j == 0, j == pl.num_programs(1) - 1, reverse=False)


def _rx_tile_specs(tm, t, tile_of):
    halo_per_tile = tm // HALO
    n_halo = t // HALO
    main = pl.BlockSpec((None, tm, D_RG), lambda b, j: (b, tile_of(j), COL_RX))
    prev = pl.BlockSpec(
        (None, HALO, D_RG), lambda b, j: (b, jnp.maximum(tile_of(j) * halo_per_tile - 1, 0), COL_RX))
    nxt = pl.BlockSpec(
        (None, HALO, D_RG), lambda b, j: (b, jnp.minimum((tile_of(j) + 1) * halo_per_tile, n_halo - 1), COL_RX))
    return [main, prev, nxt]


def _rglru_scratch(tm):
    return [pltpu.VMEM((D_RG // LANES, SUBLANES * _segment_pitch(tm // SUBLANES), LANES), F32),
            pltpu.VMEM((1, D_RG), F32)]


def _whole(a):
    return pl.BlockSpec(a.shape, lambda *_: (0,) * a.ndim)


def _rglru_fwd_scan(z3d, conv_w, conv_b, gates):
    bsz, t, _ = z3d.shape
    tm = SCAN_TILE
    params = [conv_w, conv_b, *gates]
    return pl.pallas_call(
        _rglru_fwd_kernel,
        grid=(bsz, t // tm),
        in_specs=_rx_tile_specs(tm, t, lambda j: j) + [_whole(p) for p in params],
        out_specs=pl.BlockSpec((None, tm, D_RG), lambda b, j: (b, j, 0)),
        out_shape=jax.ShapeDtypeStruct((bsz, t, D_RG), F32),
        scratch_shapes=_rglru_scratch(tm),
        compiler_params=pltpu.CompilerParams(
            dimension_semantics=("arbitrary", "arbitrary"), vmem_limit_bytes=VMEM_LIMIT),
        name="rglru_fwd_scan",
    )(z3d, z3d, z3d, *params)


def _gelu_tanh(x):
    return 0.5 * x * (1.0 + jnp.tanh(0.7978845608028654 * (x + 0.044715 * (x * x * x))))


def _mix_residual(x, gate, of_ref, ob_ref, g_ref, hf_ref, hb_ref, rgate_ref, hgain_ref, w_ref):
    o = of_ref[...] + ob_ref[...]
    g = _silu(g_ref[...].astype(F32))
    o_rg = ((hf_ref[...] + hb_ref[...]) * _gelu_tanh(rgate_ref[...].astype(F32))).astype(BF16)
    y = _dot(o_rg, w_ref[D_HG:, :])
    for h in range(HG_HEADS):
        sl = _head_lanes(h)
        o_hg = (_rms(o[:, sl], hgain_ref[...]) * g[:, sl]).astype(BF16)
        y = y + _dot(o_hg, w_ref[sl, :])
    return x + gate * y


def _mixout_kernel(x_ref, mod_ref, of_ref, ob_ref, g_ref, hf_ref, rgate_ref, rx_ref, rxprev_ref, rxnext_ref,
                   cw_ref, cb_ref, wa_ref, ba_ref, wx_ref, bx_ref, lam_ref, hgain_ref, wout_ref,
                   gain_ref, w1_ref, w3_ref, w2_ref, fgain_ref, o_ref, hb_ref, stage_ref, carry_ref):
    s = pl.program_id(1)
    n_t = pl.num_programs(1) - 1
    scan_tile = n_t - 1 - s

    def scan():
        _rglru_direction(rx_ref, rxprev_ref, rxnext_ref, cw_ref, cb_ref, wa_ref, ba_ref, wx_ref, bx_ref, lam_ref,
                         hb_ref, stage_ref, carry_ref, scan_tile == 0, scan_tile == n_t - 1, reverse=True)

    def mix_ffn():
        x2 = _mix_residual(x_ref[...], mod_ref[5:6, :], of_ref, ob_ref, g_ref, hf_ref, hb_ref,
                           rgate_ref, hgain_ref, wout_ref)
        x3 = _swiglu_residual(x2, mod_ref, 6, gain_ref, w1_ref, w3_ref, w2_ref)
        o_ref[...] = _rms(x3, fgain_ref[...])

    @pl.when(s == 0)
    def _():
        carry_ref[...] = jnp.zeros_like(carry_ref)
        scan()

    @pl.when(jnp.logical_and(s > 0, s < n_t))
    def _():
        mix_ffn()
        scan()

    @pl.when(s == n_t)
    def _():
        mix_ffn()


def _mixout(x3d, mod, o_f, o_b, z3d, h_f, conv_w, conv_b, gates_bwd, hg_gain, w_out, gain, w1, w3, w2, final_gain):
    bsz, t, d = x3d.shape
    tm = ROW_TILE
    n_t = t // tm
    ffn_tile = lambda s: jnp.minimum(n_t - s, n_t - 1)
    scan_tile = lambda s: jnp.maximum(n_t - 1 - s, 0)

    def rows(width, col=0):
        return pl.BlockSpec((None, tm, width), lambda b, s: (b, ffn_tile(s), col))

    scan_params = [conv_w, conv_b, *gates_bwd]
    ffn_params = [hg_gain, w_out, gain, w1, w3, w2, final_gain]
    return pl.pallas_call(
        _mixout_kernel,
        grid=(bsz, n_t + 1),
        in_specs=[rows(d), pl.BlockSpec((None, N_MOD, d), lambda b, s: (b, 0, 0)),
                  rows(D_HG), rows(D_HG), rows(D_HG, COL_G), rows(D_RG), rows(D_RG, COL_RGATE),
                  *_rx_tile_specs(tm, t, scan_tile),
                  *[_whole(p) for p in scan_params], *[_resident(p.shape) for p in ffn_params]],
        out_specs=rows(d),
        out_shape=jax.ShapeDtypeStruct((bsz, t, d), F32),
        scratch_shapes=[pltpu.VMEM((tm, D_RG), F32), *_rglru_scratch(tm)],
        compiler_params=pltpu.CompilerParams(
            dimension_semantics=("arbitrary", "arbitrary"), vmem_limit_bytes=VMEM_LIMIT),
        name="mixout",
    )(x3d, mod, o_f, o_b, z3d, h_f, z3d, z3d, z3d, z3d, *scan_params, *ffn_params)


def _block_diag(w):
    nb, d, e = w.shape
    eye = jnp.eye(nb, dtype=w.dtype)
    return (eye[:, None, :, None] * w[:, :, None, :]).reshape(nb * d, nb * e)


def kernel(x, c, w_ada, b_ada, ffn1_norm, ffn1_w1, ffn1_w3, ffn1_w2, mix_norm, w_in, hg_lb_fwd, hg_lb_bwd, hg_out_norm, rg_conv_w, rg_conv_b, rg_wa_fwd, rg_ba_fwd, rg_wx_fwd, rg_bx_fwd, rg_lam_fwd, rg_wa_bwd, rg_ba_bwd, rg_wx_bwd, rg_bx_bwd, rg_lam_bwd, w_out, ffn2_norm, ffn2_w1, ffn2_w3, ffn2_w2, final_norm):
    bsz, t, d = x.shape
    depth = w_ada.shape[0]
    assert depth == 1 and hg_lb_fwd.shape[0] == 2, "kernels are written for a single layer"
    assert t % ROW_TILE == 0 and SCAN_TILE == ROW_TILE and SCAN_TILE % HG_CHUNK == 0
    tiles_per_batch = t // ROW_TILE
    row = lambda v: v.reshape(1, -1)
    l = 0

    mod = _modulation(c, w_ada[l], b_ada[l])
    bf16 = lambda w: w[l].astype(BF16)
    flat = lambda a: a.reshape(bsz * t, a.shape[-1])

    x1, z2d = _ffn_inproj(flat(x), mod, row(ffn1_norm[l]), bf16(ffn1_w1), bf16(ffn1_w3), bf16(ffn1_w2),
                          row(mix_norm[l]), bf16(w_in), tiles_per_batch=tiles_per_batch)
    z3d = z2d.reshape(bsz, t, IN_COLS)
    o_f, o_b = _hgrn_scan(z3d, hg_lb_fwd, hg_lb_bwd)

    def gates(wa, ba, wx, bx, lam):
        return (_block_diag(wa[l]).astype(BF16), row(ba[l]), _block_diag(wx[l]).astype(BF16), row(bx[l]),
                row(lam[l]))

    conv_w, conv_b = rg_conv_w[l], row(rg_conv_b[l])
    h_f = _rglru_fwd_scan(z3d, conv_w, conv_b, gates(rg_wa_fwd, rg_ba_fwd, rg_wx_fwd, rg_bx_fwd, rg_lam_fwd))

    return _mixout(x1.reshape(bsz, t, d), mod, o_f, o_b, z3d, h_f, conv_w, conv_b,
                   gates(rg_wa_bwd, rg_ba_bwd, rg_wx_bwd, rg_bx_bwd, rg_lam_bwd), row(hg_out_norm[l]), bf16(w_out),
                   row(ffn2_norm[l]), bf16(ffn2_w1), bf16(ffn2_w3), bf16(ffn2_w2), row(final_norm))
```

```python
import math
from typing import NamedTuple

import jax
import jax.numpy as jnp
from jax import lax
from jax.experimental import pallas as pl
from jax.experimental.pallas import tpu as pltpu

F32 = jnp.float32
BF16 = jnp.bfloat16
Z_DTYPE = F32

EPS = 1e-6
N_MOD = 9
HG_HEAD_DIM = 128
HG_HEADS = 4
D_HG = HG_HEADS * HG_HEAD_DIM
RG_BLOCKS = 8
D_RG = 512
RG_CONV_W = 4
RG_CONV_LEFT = RG_CONV_W // 2
RG_C = 8.0
IN_COLS = 5 * D_HG + 2 * D_RG
COL_Q, COL_F_FWD, COL_F_BWD, COL_I, COL_G, COL_RX, COL_RGATE = range(7)

SUBLANES = 8
LANES = 128
HALO = 2 * SUBLANES
HG_CHUNK = 64
LOG2_E = math.log2(math.e)
HG_SAFE_EXPONENT_LOG2 = 115.0
VMEM_LIMIT = 56 * 1024 * 1024

ROW_TILE = 256
SCAN_TILE = 256
HGRN_TILE = 512


def _rms(x, gain):
    return x * lax.rsqrt(jnp.mean(x * x, axis=-1, keepdims=True) + EPS) * gain


def _norm_mod(x, gain, shift, scale):
    return _rms(x, gain) * (1.0 + scale) + shift


def _silu(x):
    return x * jax.nn.sigmoid(x)


def _dot(a, b):
    return jnp.dot(a, b, preferred_element_type=F32)


def _dot_nt(a, b):
    return lax.dot_general(a, b, (((1,), (1,)), ((), ())), preferred_element_type=F32)


def _dot_tn(a, b):
    return lax.dot_general(a, b, (((0,), (0,)), ((), ())), preferred_element_type=F32)


def _mod_kernel(c_ref, w_ref, b_ref, o_ref):
    o_ref[...] = _dot(_silu(c_ref[...]), w_ref[...]) + b_ref[...]


def _modulation(c, w_ada, b_ada):
    bsz, d = c.shape
    n = w_ada.shape[1]
    bn = 1024
    c_pad = jnp.pad(c, ((0, SUBLANES - bsz), (0, 0)))
    out = pl.pallas_call(
        _mod_kernel,
        grid=(n // bn,),
        in_specs=[
            pl.BlockSpec((SUBLANES, d), lambda j: (0, 0)),
            pl.BlockSpec((d, bn), lambda j: (0, j)),
            pl.BlockSpec((1, bn), lambda j: (0, j)),
        ],
        out_specs=pl.BlockSpec((SUBLANES, bn), lambda j: (0, j)),
        out_shape=jax.ShapeDtypeStruct((SUBLANES, n), F32),
        name="modulation",
    )(c_pad, w_ada, b_ada.reshape(1, n))
    return out[:bsz].reshape(bsz, N_MOD, d)


def _swiglu_residual(x, mod_ref, mod_base, gain_ref, w1_ref, w3_ref, w2_ref):
    shift = mod_ref[mod_base:mod_base + 1, :]
    scale = mod_ref[mod_base + 1:mod_base + 2, :]
    gate = mod_ref[mod_base + 2:mod_base + 3, :]
    h = _norm_mod(x, gain_ref[...], shift, scale).astype(BF16)
    a = _dot(h, w1_ref[...])
    b = _dot(h, w3_ref[...])
    g = (_silu(a) * b).astype(BF16)
    return x + (0.5 * gate) * _dot(g, w2_ref[...])


def _ffn_inproj_kernel(x_ref, mod_ref, gain_ref, w1_ref, w3_ref, w2_ref, mix_gain_ref, win_ref, x1_ref, z_ref):
    x1 = _swiglu_residual(x_ref[...], mod_ref, 0, gain_ref, w1_ref, w3_ref, w2_ref)
    x1_ref[...] = x1
    h = _norm_mod(x1, mix_gain_ref[...], mod_ref[3:4, :], mod_ref[4:5, :]).astype(BF16)
    z_ref[...] = _dot(h, win_ref[...]).astype(z_ref.dtype)


def _resident(shape):
    return pl.BlockSpec(shape, lambda *_: (0,) * len(shape), pipeline_mode=pl.Buffered(1))


def _row_tiles(tm, width, col=0):
    return pl.BlockSpec((tm, width), lambda i: (i, col))


def _mod_rows(d, tiles_per_batch):
    return pl.BlockSpec((None, N_MOD, d), lambda i: (i // tiles_per_batch, 0, 0))


def _ffn_inproj(x2d, mod, gain, w1, w3, w2, mix_gain, w_in, *, tiles_per_batch):
    n, d = x2d.shape
    cols = w_in.shape[1]
    tm = ROW_TILE
    return pl.pallas_call(
        _ffn_inproj_kernel,
        grid=(n // tm,),
        in_specs=[_row_tiles(tm, d), _mod_rows(d, tiles_per_batch), _resident(gain.shape), _resident(w1.shape),
                  _resident(w3.shape), _resident(w2.shape), _resident(mix_gain.shape), _resident(w_in.shape)],
        out_specs=[_row_tiles(tm, d), _row_tiles(tm, cols)],
        out_shape=[jax.ShapeDtypeStruct((n, d), F32), jax.ShapeDtypeStruct((n, cols), Z_DTYPE)],
        compiler_params=pltpu.CompilerParams(
            dimension_semantics=("arbitrary",), vmem_limit_bytes=VMEM_LIMIT),
        name="ffn_inproj",
    )(x2d, mod, gain, w1, w3, w2, mix_gain, w_in)


def _lower_bound(lb_ref):
    p0 = lb_ref[0:1, :]
    p1 = lb_ref[1:2, :]
    m = jnp.maximum(p0, p1)
    e0 = jnp.exp(p0 - m)
    e1 = jnp.exp(p1 - m)
    return e1 / (e0 + e1)


def _split3(x):
    hi = x.astype(BF16)
    r = x - hi.astype(F32)
    mid = r.astype(BF16)
    lo = (r - mid.astype(F32)).astype(BF16)
    return hi, mid, lo


class _HgrnScratch(NamedTuple):
    q: object
    k: object
    lg3: object
    b: object
    v16: object
    qm: object
    km: object
    qi: object
    ks: object
    ds: object
    dec: object
    wide: object


def _chunk_rows(c):
    return slice(c * HG_CHUNK, (c + 1) * HG_CHUNK)


def _head_lanes(h):
    return slice(h * HG_HEAD_DIM, (h + 1) * HG_HEAD_DIM)


def _chunk_marks(reverse):
    c = HG_CHUNK
    return (c - 1, 0, c // 2) if reverse else (0, c - 1, c // 2 - 1)


def _hgrn_elementwise(q_ref, f_ref, v_ref, lb, sc):
    sc.q[...] = _silu(q_ref[...].astype(F32))
    f = lb + (1.0 - lb) * jax.nn.sigmoid(f_ref[...].astype(F32))
    sc.k[...] = 1.0 - f
    hi, mid, lo = _split3(jnp.log(f))
    sc.lg3[:, 0:D_HG] = hi
    sc.lg3[:, D_HG:2 * D_HG] = mid
    sc.lg3[:, 2 * D_HG:3 * D_HG] = lo
    sc.v16[...] = v_ref[...].astype(BF16)


def _hgrn_cumsum(sc, tri, n_chunks):
    for c in range(n_chunks):
        rows = _chunk_rows(c)
        p = _dot(tri, sc.lg3[rows, :])
        sc.b[rows, :] = p[:, 0:D_HG] + p[:, D_HG:2 * D_HG] + p[:, 2 * D_HG:3 * D_HG]


def _hgrn_factors(sc, n_chunks, *, reverse):
    first, last, ref = _chunk_marks(reverse)
    n_wide = None
    for c in range(n_chunks):
        rows = _chunk_rows(c)
        b2 = sc.b[rows, :] * LOG2_E
        r2 = b2[ref:ref + 1, :]
        end2 = b2[last:last + 1, :]
        spread = jnp.maximum(b2[first:first + 1, :] - r2, r2 - end2)
        wide = jnp.where(spread < HG_SAFE_EXPONENT_LOG2, 0.0, 1.0)
        sc.wide[c:c + 1, :] = wide
        n_wide = wide if n_wide is None else n_wide + wide
        narrow = jnp.broadcast_to(wide, b2.shape) == 0.0
        q, k = sc.q[rows, :], sc.k[rows, :]
        sc.qm[rows, :] = jnp.where(narrow, q * jnp.exp2(b2 - r2), 0.0).astype(BF16)
        sc.km[rows, :] = jnp.where(narrow, k * jnp.exp2(r2 - b2), 0.0).astype(BF16)
        sc.qi[rows, :] = (q * jnp.exp2(b2)).astype(BF16)
        sc.ks[rows, :] = (k * jnp.exp2(end2 - b2)).astype(BF16)
        sc.dec[c:c + 1, :] = jnp.exp2(end2)
    return n_wide


def _hgrn_state_increment(sc, c, h):
    rows, sl = _chunk_rows(c), _head_lanes(h)
    sc.ds[c, h] = _dot_tn(sc.v16[rows, sl], sc.ks[rows, sl])


def _hgrn_intra_factored(sc, o_ref, tri, n_chunks):
    causal = tri > 0
    pairs = [(c, h) for c in range(n_chunks) for h in range(HG_HEADS)]
    for c, h in pairs:
        _hgrn_state_increment(sc, c, h)
    scores = {}
    for c, h in pairs:
        rows, sl = _chunk_rows(c), _head_lanes(h)
        scores[c, h] = jnp.where(causal, _dot_nt(sc.qm[rows, sl], sc.km[rows, sl]), 0.0).astype(BF16)
    for c, h in pairs:
        rows, sl = _chunk_rows(c), _head_lanes(h)
        o_ref[rows, sl] = _dot(scores[c, h], sc.v16[rows, sl])


def _hgrn_intra_halving(sc, o_ref, n_chunks, *, reverse):
    cs = HG_CHUNK
    row = lax.broadcasted_iota(jnp.int32, (cs, cs), 0)
    col = lax.broadcasted_iota(jnp.int32, (cs, cs), 1)
    pos = lax.broadcasted_iota(jnp.int32, (cs, D_HG), 0)
    for c in range(n_chunks):
        rows = _chunk_rows(c)
        wide = jnp.broadcast_to(sc.wide[c:c + 1, :], (cs, D_HG)) > 0.0
        b, k = sc.b[rows, :], sc.k[rows, :]
        q = jnp.where(wide, sc.q[rows, :], 0.0)
        q16, k16 = q.astype(BF16), k.astype(BF16)
        scores = [jnp.where(row == col, _dot_nt(q16[:, _head_lanes(h)], k16[:, _head_lanes(h)]), 0.0)
                  for h in range(HG_HEADS)]
        b_hi, b_mid, b_lo = _split3(b)
        shift = 1
        while (1 << shift) <= cs:
            blk, half = 1 << shift, 1 << (shift - 1)
            boundary = ((row >> shift) << shift) + (half if reverse else half - 1)
            pick = jnp.where(col == boundary, 1.0, 0.0).astype(BF16)
            b_bnd = _dot(pick, b_hi) + _dot(pick, b_mid) + _dot(pick, b_lo)
            in_half = (pos & (blk - 1)) >= half
            attends = jnp.logical_not(in_half) if reverse else in_half
            qh = jnp.where(attends, q * jnp.exp(b - b_bnd), 0.0).astype(BF16)
            kh = jnp.where(attends, 0.0, k * jnp.exp(b_bnd - b)).astype(BF16)
            same_block = (row >> shift) == (col >> shift)
            for h in range(HG_HEADS):
                sl = _head_lanes(h)
                scores[h] = scores[h] + jnp.where(same_block, _dot_nt(qh[:, sl], kh[:, sl]), 0.0)
            shift += 1
        for h in range(HG_HEADS):
            sl = _head_lanes(h)
            o_ref[rows, sl] += _dot(scores[h].astype(BF16), sc.v16[rows, sl])


def _hgrn_carry_state(sc, o_ref, s_ref, n_chunks, *, reverse):
    order = range(n_chunks - 1, -1, -1) if reverse else range(n_chunks)
    for c in order:
        rows = _chunk_rows(c)
        for h in range(HG_HEADS):
            sl = _head_lanes(h)
            st = s_ref[h]
            o_ref[rows, sl] += _dot_nt(sc.qi[rows, sl], st.astype(BF16))
            s_ref[h] = st * sc.dec[c:c + 1, sl] + sc.ds[c, h]


def _hgrn_kernel(qf_ref, ff_ref, vf_ref, qb_ref, fb_ref, vb_ref, lbf_ref, lbb_ref, of_ref, ob_ref, s_ref, *scratch):
    tm = of_ref.shape[0]
    n_chunks = tm // HG_CHUNK
    sc_f = _HgrnScratch(*(r.at[0] for r in scratch))
    sc_b = _HgrnScratch(*(r.at[1] for r in scratch))

    @pl.when(pl.program_id(1) == 0)
    def _():
        s_ref[...] = jnp.zeros_like(s_ref)

    row = lax.broadcasted_iota(jnp.int32, (HG_CHUNK, HG_CHUNK), 0)
    col = lax.broadcasted_iota(jnp.int32, (HG_CHUNK, HG_CHUNK), 1)
    tri_f = jnp.where(col <= row, 1.0, 0.0).astype(BF16)
    tri_b = jnp.where(col >= row, 1.0, 0.0).astype(BF16)

    _hgrn_elementwise(qf_ref, ff_ref, vf_ref, _lower_bound(lbf_ref), sc_f)
    _hgrn_elementwise(qb_ref, fb_ref, vb_ref, _lower_bound(lbb_ref), sc_b)
    _hgrn_cumsum(sc_f, tri_f, n_chunks)
    _hgrn_cumsum(sc_b, tri_b, n_chunks)
    n_wide = _hgrn_factors(sc_f, n_chunks, reverse=False) + _hgrn_factors(sc_b, n_chunks, reverse=True)
    _hgrn_intra_factored(sc_f, of_ref, tri_f, n_chunks)
    _hgrn_intra_factored(sc_b, ob_ref, tri_b, n_chunks)
    _hgrn_carry_state(sc_f, of_ref, s_ref.at[0], n_chunks, reverse=False)
    _hgrn_carry_state(sc_b, ob_ref, s_ref.at[1], n_chunks, reverse=True)

    @pl.when(jnp.max(n_wide) > 0.0)
    def _():
        _hgrn_intra_halving(sc_f, of_ref, n_chunks, reverse=False)
        _hgrn_intra_halving(sc_b, ob_ref, n_chunks, reverse=True)


def _hgrn_scan(z3d, lb_fwd, lb_bwd):
    bsz, t, _ = z3d.shape
    tm = HGRN_TILE
    n_t = t // tm
    n_chunks = tm // HG_CHUNK

    def fwd(col):
        return pl.BlockSpec((None, tm, D_HG), lambda b, j: (b, j, col))

    def bwd(col):
        return pl.BlockSpec((None, tm, D_HG), lambda b, j: (b, n_t - 1 - j, col))

    lb_spec = pl.BlockSpec(lb_fwd.shape, lambda b, j: (0, 0))
    tile = lambda width, dtype: pltpu.VMEM((2, tm, width), dtype)
    scratch = _HgrnScratch(
        q=tile(D_HG, F32), k=tile(D_HG, F32), lg3=tile(3 * D_HG, BF16), b=tile(D_HG, F32), v16=tile(D_HG, BF16),
        qm=tile(D_HG, BF16), km=tile(D_HG, BF16), qi=tile(D_HG, BF16), ks=tile(D_HG, BF16),
        ds=pltpu.VMEM((2, n_chunks, HG_HEADS, HG_HEAD_DIM, HG_HEAD_DIM), F32),
        dec=pltpu.VMEM((2, n_chunks, D_HG), F32), wide=pltpu.VMEM((2, n_chunks, D_HG), F32))
    return pl.pallas_call(
        _hgrn_kernel,
        grid=(bsz, n_t),
        in_specs=[fwd(COL_Q), fwd(COL_F_FWD), fwd(COL_I), bwd(COL_Q), bwd(COL_F_BWD), bwd(COL_I),
                  lb_spec, lb_spec],
        out_specs=[pl.BlockSpec((None, tm, D_HG), lambda b, j: (b, j, 0)),
                   pl.BlockSpec((None, tm, D_HG), lambda b, j: (b, n_t - 1 - j, 0))],
        out_shape=[jax.ShapeDtypeStruct((bsz, t, D_HG), F32)] * 2,
        scratch_shapes=[pltpu.VMEM((2, HG_HEADS, HG_HEAD_DIM, HG_HEAD_DIM), F32), *scratch],
        compiler_params=pltpu.CompilerParams(
            dimension_semantics=("arbitrary", "arbitrary"), vmem_limit_bytes=VMEM_LIMIT),
        name="hgrn_scan",
    )(z3d, z3d, z3d, z3d, z3d, z3d, lb_fwd, lb_bwd)


def _softplus(x):
    return jnp.maximum(x, 0.0) + jnp.log1p(jnp.exp(-jnp.abs(x)))


def _lane_groups(width):
    return [slice(g * LANES, (g + 1) * LANES) for g in range(width // LANES)]


def _segment_pitch(seg_len):
    assert seg_len % (2 * SUBLANES) == 0
    return seg_len + SUBLANES


def _segment_rows(stage_ref, i, seg_len):
    rows = pl.ds(i, SUBLANES, stride=_segment_pitch(seg_len))
    return jnp.concatenate([stage_ref[g, rows, :] for g in range(stage_ref.shape[0])], axis=1)


def _stage_segments(stage_ref, x_ref, seg_len):
    pitch = _segment_pitch(seg_len)
    for g, lanes in enumerate(_lane_groups(x_ref.shape[1])):
        for s in range(SUBLANES):
            stage_ref[g, s * pitch:s * pitch + seg_len, :] = x_ref[s * seg_len:(s + 1) * seg_len, lanes].astype(F32)


def _unstage_segments(h_ref, stage_ref, seg_len):
    pitch = _segment_pitch(seg_len)
    for g, lanes in enumerate(_lane_groups(h_ref.shape[1])):
        for s in range(SUBLANES):
            h_ref[s * seg_len:(s + 1) * seg_len, lanes] = stage_ref[g, s * pitch:s * pitch + seg_len, :]


def _shift_segments(v, fill_row, *, down):
    sub = lax.broadcasted_iota(jnp.int32, v.shape, 0)
    if down:
        return jnp.where(sub >= 1, pltpu.roll(v, 1, 0), fill_row)
    return jnp.where(sub < SUBLANES - 1, pltpu.roll(v, SUBLANES - 1, 0), fill_row)


class _SegmentedScan:
    def __init__(self, n, reverse):
        self.n, self.reverse = n, reverse
        self.h, self.a_cum, self.last = [None] * n, [None] * n, None

    def order(self):
        return list(range(self.n - 1, -1, -1) if self.reverse else range(self.n))

    def feed(self, i, a, u):
        if self.last is None:
            self.h[i], self.a_cum[i] = u, a
        else:
            self.h[i], self.a_cum[i] = a * self.h[self.last] + u, a * self.a_cum[self.last]
        self.last = i

    def finish(self, carry):
        h_end, a_end = self.h[self.last], self.a_cum[self.last]
        entering = [None] * SUBLANES
        for s in (range(SUBLANES - 1, -1, -1) if self.reverse else range(SUBLANES)):
            entering[s] = carry
            carry = h_end[s:s + 1, :] + a_end[s:s + 1, :] * carry
        entering = jnp.concatenate(entering, axis=0)
        return [self.h[i] + self.a_cum[i] * entering for i in range(self.n)], carry


def _rglru_conv_gates(x_ref, prev_ref, next_ref, cw_ref, cb_ref, wa_ref, wx_ref, stage_ref, first_tile, last_tile):
    tm = x_ref.shape[0]
    n = tm // SUBLANES
    _stage_segments(stage_ref, x_ref, n)
    xs = [_segment_rows(stage_ref, i, n) for i in range(n)]
    prev = jnp.where(first_tile, 0.0, prev_ref[...].astype(F32))
    nxt = jnp.where(last_tile, 0.0, next_ref[...].astype(F32))
    edge = {-2: _shift_segments(xs[n - 2], prev[HALO - 2:HALO - 1, :], down=True),
            -1: _shift_segments(xs[n - 1], prev[HALO - 1:HALO, :], down=True),
            n: _shift_segments(xs[0], nxt[0:1, :], down=False)}
    tap = lambda i: xs[i] if 0 <= i < n else edge[i]
    cw = [cw_ref[w:w + 1, :] for w in range(RG_CONV_W)]
    cb = cb_ref[...]
    xc = []
    for i in range(n):
        acc = cb + tap(i - RG_CONV_LEFT) * cw[0]
        for w in range(1, RG_CONV_W):
            acc = acc + tap(i - RG_CONV_LEFT + w) * cw[w]
        xc.append(acc)
    xc = jnp.concatenate(xc, axis=0)
    xc16 = xc.astype(BF16)
    return xc, _dot(xc16, wa_ref[...]), _dot(xc16, wx_ref[...])


def _rglru_recurrence(xc, pre_r, pre_i, ba_ref, bx_ref, lam_ref, h_ref, stage_ref, carry_ref, *, reverse):
    n = xc.shape[0] // SUBLANES
    r = jax.nn.sigmoid(pre_r + ba_ref[...])
    gate_i = jax.nn.sigmoid(pre_i + bx_ref[...])
    log_a = (-RG_C) * r * _softplus(-lam_ref[...])
    a = jnp.exp(log_a)
    y = -jnp.tanh(log_a) * (a * a + 1.0)
    u = jnp.where(y > 0.0, y * lax.rsqrt(y), 0.0) * (gate_i * xc)
    scan = _SegmentedScan(n, reverse)
    for i in scan.order():
        vec = slice(i * SUBLANES, (i + 1) * SUBLANES)
        scan.feed(i, a[vec, :], u[vec, :])
    h, carry = scan.finish(carry_ref[...])
    carry_ref[...] = carry
    for i in range(n):
        for g, lanes in enumerate(_lane_groups(D_RG)):
            stage_ref[g, pl.ds(i, SUBLANES, stride=_segment_pitch(n)), :] = h[i][:, lanes]
    _unstage_segments(h_ref, stage_ref, n)


def _rglru_direction(x_ref, prev_ref, next_ref, cw_ref, cb_ref, wa_ref, ba_ref, wx_ref, bx_ref,
                     lam_ref, h_ref, stage_ref, carry_ref, first_tile, last_tile, *, reverse):
    pre = _rglru_conv_gates(x_ref, prev_ref, next_ref, cw_ref, cb_ref, wa_ref, wx_ref, stage_ref,
                            first_tile, last_tile)
    _rglru_recurrence(*pre, ba_ref, bx_ref, lam_ref, h_ref, stage_ref, carry_ref, reverse=reverse)


def _rglru_fwd_kernel(x_ref, prev_ref, next_ref, cw_ref, cb_ref, wa_ref, ba_ref, wx_ref, bx_ref, lam_ref,
                      h_ref, stage_ref, carry_ref):
    j = pl.program_id(1)

    @pl.when(j == 0)
    def _():
        carry_ref[...] = jnp.zeros_like(carry_ref)

    _rglru_direction(x_ref, prev_ref, next_ref, cw_ref, cb_ref, wa_ref, ba_ref, wx_ref, bx_ref, lam_ref,
                     h_ref, stage_ref, carry_ref, j == 0, j == pl.num_programs(1) - 1, reverse=False)


def _rx_tile_specs(tm, t, tile_of):
    halo_per_tile = tm // HALO
    n_halo = t // HALO
    main = pl.BlockSpec((None, tm, D_RG), lambda b, j: (b, tile_of(j), COL_RX))
    prev = pl.BlockSpec(
        (None, HALO, D_RG), lambda b, j: (b, jnp.maximum(tile_of(j) * halo_per_tile - 1, 0), COL_RX))
    nxt = pl.BlockSpec(
        (None, HALO, D_RG), lambda b, j: (b, jnp.minimum((tile_of(j) + 1) * halo_per_tile, n_halo - 1), COL_RX))
    return [main, prev, nxt]


def _rglru_scratch(tm):
    return [pltpu.VMEM((D_RG // LANES, SUBLANES * _segment_pitch(tm // SUBLANES), LANES), F32),
            pltpu.VMEM((1, D_RG), F32)]


def _whole(a):
    return pl.BlockSpec(a.shape, lambda *_: (0,) * a.ndim)


def _rglru_fwd_scan(z3d, conv_w, conv_b, gates):
    bsz, t, _ = z3d.shape
    tm = SCAN_TILE
    params = [conv_w, conv_b, *gates]
    return pl.pallas_call(
        _rglru_fwd_kernel,
        grid=(bsz, t // tm),
        in_specs=_rx_tile_specs(tm, t, lambda j: j) + [_whole(p) for p in params],
        out_specs=pl.BlockSpec((None, tm, D_RG), lambda b, j: (b, j, 0)),
        out_shape=jax.ShapeDtypeStruct((bsz, t, D_RG), F32),
        scratch_shapes=_rglru_scratch(tm),
        compiler_params=pltpu.CompilerParams(
            dimension_semantics=("arbitrary", "arbitrary"), vmem_limit_bytes=VMEM_LIMIT),
        name="rglru_fwd_scan",
    )(z3d, z3d, z3d, *params)


def _gelu_tanh(x):
    return 0.5 * x * (1.0 + jnp.tanh(0.7978845608028654 * (x + 0.044715 * (x * x * x))))


def _mix_residual(x, gate, of_ref, ob_ref, g_ref, hf_ref, hb_ref, rgate_ref, hgain_ref, w_ref):
    o = of_ref[...] + ob_ref[...]
    g = _silu(g_ref[...].astype(F32))
    o_rg = ((hf_ref[...] + hb_ref[...]) * _gelu_tanh(rgate_ref[...].astype(F32))).astype(BF16)
    y = _dot(o_rg, w_ref[D_HG:, :])
    for h in range(HG_HEADS):
        sl = _head_lanes(h)
        o_hg = (_rms(o[:, sl], hgain_ref[...]) * g[:, sl]).astype(BF16)
        y = y + _dot(o_hg, w_ref[sl, :])
    return x + gate * y


def _mixout_kernel(x_ref, mod_ref, of_ref, ob_ref, g_ref, hf_ref, rgate_ref, rx_ref, rxprev_ref, rxnext_ref,
                   cw_ref, cb_ref, wa_ref, ba_ref, wx_ref, bx_ref, lam_ref, hgain_ref, wout_ref,
                   gain_ref, w1_ref, w3_ref, w2_ref, fgain_ref, o_ref, hb_ref, stage_ref, carry_ref):
    s = pl.program_id(1)
    n_t = pl.num_programs(1) - 1
    scan_tile = n_t - 1 - s

    def scan():
        _rglru_direction(rx_ref, rxprev_ref, rxnext_ref, cw_ref, cb_ref, wa_ref, ba_ref, wx_ref, bx_ref, lam_ref,
                         hb_ref, stage_ref, carry_ref, scan_tile == 0, scan_tile == n_t - 1, reverse=True)

    def mix_ffn():
        x2 = _mix_residual(x_ref[...], mod_ref[5:6, :], of_ref, ob_ref, g_ref, hf_ref, hb_ref,
                           rgate_ref, hgain_ref, wout_ref)
        x3 = _swiglu_residual(x2, mod_ref, 6, gain_ref, w1_ref, w3_ref, w2_ref)
        o_ref[...] = _rms(x3, fgain_ref[...])

    @pl.when(s == 0)
    def _():
        carry_ref[...] = jnp.zeros_like(carry_ref)
        scan()

    @pl.when(jnp.logical_and(s > 0, s < n_t))
    def _():
        mix_ffn()
        scan()

    @pl.when(s == n_t)
    def _():
        mix_ffn()


def _mixout(x3d, mod, o_f, o_b, z3d, h_f, conv_w, conv_b, gates_bwd, hg_gain, w_out, gain, w1, w3, w2, final_gain):
    bsz, t, d = x3d.shape
    tm = ROW_TILE
    n_t = t // tm
    ffn_tile = lambda s: jnp.minimum(n_t - s, n_t - 1)
    scan_tile = lambda s: jnp.maximum(n_t - 1 - s, 0)

    def rows(width, col=0):
        return pl.BlockSpec((None, tm, width), lambda b, s: (b, ffn_tile(s), col))

    scan_params = [conv_w, conv_b, *gates_bwd]
    ffn_params = [hg_gain, w_out, gain, w1, w3, w2, final_gain]
    return pl.pallas_call(
        _mixout_kernel,
        grid=(bsz, n_t + 1),
        in_specs=[rows(d), pl.BlockSpec((None, N_MOD, d), lambda b, s: (b, 0, 0)),
                  rows(D_HG), rows(D_HG), rows(D_HG, COL_G), rows(D_RG), rows(D_RG, COL_RGATE),
                  *_rx_tile_specs(tm, t, scan_tile),
                  *[_whole(p) for p in scan_params], *[_resident(p.shape) for p in ffn_params]],
        out_specs=rows(d),
        out_shape=jax.ShapeDtypeStruct((bsz, t, d), F32),
        scratch_shapes=[pltpu.VMEM((tm, D_RG), F32), *_rglru_scratch(tm)],
        compiler_params=pltpu.CompilerParams(
            dimension_semantics=("arbitrary", "arbitrary"), vmem_limit_bytes=VMEM_LIMIT),
        name="mixout",
    )(x3d, mod, o_f, o_b, z3d, h_f, z3d, z3d, z3d, z3d, *scan_params, *ffn_params)


def _block_diag(w):
    nb, d, e = w.shape
    eye = jnp.eye(nb, dtype=w.dtype)
    return (eye[:, None, :, None] * w[:, :, None, :]).reshape(nb * d, nb * e)


def kernel(x, c, w_ada, b_ada, ffn1_norm, ffn1_w1, ffn1_w3, ffn1_w2, mix_norm, w_in, hg_lb_fwd, hg_lb_bwd, hg_out_norm, rg_conv_w, rg_conv_b, rg_wa_fwd, rg_ba_fwd, rg_wx_fwd, rg_bx_fwd, rg_lam_fwd, rg_wa_bwd, rg_ba_bwd, rg_wx_bwd, rg_bx_bwd, rg_lam_bwd, w_out, ffn2_norm, ffn2_w1, ffn2_w3, ffn2_w2, final_norm):
    bsz, t, d = x.shape
    depth = w_ada.shape[0]
    assert depth == 1 and hg_lb_fwd.shape[0] == 2, "kernels are written for a single layer"
    assert t % ROW_TILE == 0 and SCAN_TILE == ROW_TILE and t % HGRN_TILE == 0 and HGRN_TILE % HG_CHUNK == 0
    tiles_per_batch = t // ROW_TILE
    row = lambda v: v.reshape(1, -1)
    l = 0

    mod = _modulation(c, w_ada[l], b_ada[l])
    bf16 = lambda w: w[l].astype(BF16)
    flat = lambda a: a.reshape(bsz * t, a.shape[-1])

    x1, z2d = _ffn_inproj(flat(x), mod, row(ffn1_norm[l]), bf16(ffn1_w1), bf16(ffn1_w3), bf16(ffn1_w2),
                          row(mix_norm[l]), bf16(w_in), tiles_per_batch=tiles_per_batch)
    z3d = z2d.reshape(bsz, t, IN_COLS)
    o_f, o_b = _hgrn_scan(z3d, hg_lb_fwd, hg_lb_bwd)

    def gates(wa, ba, wx, bx, lam):
        return (_block_diag(wa[l]).astype(BF16), row(ba[l]), _block_diag(wx[l]).astype(BF16), row(bx[l]),
                row(lam[l]))

    conv_w, conv_b = rg_conv_w[l], row(rg_conv_b[l])
    h_f = _rglru_fwd_scan(z3d, conv_w, conv_b, gates(rg_wa_fwd, rg_ba_fwd, rg_wx_fwd, rg_bx_fwd, rg_lam_fwd))

    return _mixout(x1.reshape(bsz, t, d), mod, o_f, o_b, z3d, h_f, conv_w, conv_b,
                   gates(rg_wa_bwd, rg_ba_bwd, rg_wx_bwd, rg_bx_bwd, rg_lam_bwd), row(hg_out_norm[l]), bf16(w_out),
                   row(ffn2_norm[l]), bf16(ffn2_w1), bf16(ffn2_w3), bf16(ffn2_w2), row(final_norm))
```

```python
import math
from typing import NamedTuple

import jax
import jax.numpy as jnp
from jax import lax
from jax.experimental import pallas as pl
from jax.experimental.pallas import tpu as pltpu

F32 = jnp.float32
BF16 = jnp.bfloat16

EPS = 1e-6
N_MOD = 9
HG_HEAD_DIM = 128
HG_HEADS = 4
D_HG = HG_HEADS * HG_HEAD_DIM
RG_BLOCKS = 8
D_RG = 512
RG_CONV_W = 4
RG_CONV_LEFT = RG_CONV_W // 2
RG_C = 8.0
IN_COLS = 5 * D_HG + 2 * D_RG
COL_Q, COL_F_FWD, COL_F_BWD, COL_I, COL_G, COL_RX, COL_RGATE = range(7)

SUBLANES = 8
LANES = 128
HALO = SUBLANES
HG_CHUNK = 64
LOG2_E = math.log2(math.e)
HG_SAFE_EXPONENT_LOG2 = 115.0
VMEM_LIMIT = 56 * 1024 * 1024

ROW_TILE = 256
RGLRU_FWD_TILE = 512
HGRN_TILE = 512


def _rms(x, gain):
    return x * lax.rsqrt(jnp.mean(x * x, axis=-1, keepdims=True) + EPS) * gain


def _norm_mod(x, gain, shift, scale):
    return _rms(x, gain) * (1.0 + scale) + shift


def _silu(x):
    return x * jax.nn.sigmoid(x)


def _dot(a, b):
    return jnp.dot(a, b, preferred_element_type=F32)


def _dot_nt(a, b):
    return lax.dot_general(a, b, (((1,), (1,)), ((), ())), preferred_element_type=F32)


def _dot_tn(a, b):
    return lax.dot_general(a, b, (((0,), (0,)), ((), ())), preferred_element_type=F32)


def _mod_kernel(c_ref, w_ref, b_ref, o_ref):
    o_ref[...] = _dot(_silu(c_ref[...]), w_ref[...]) + b_ref[...]


def _modulation(c, w_ada, b_ada):
    bsz, d = c.shape
    n = w_ada.shape[1]
    bn = 1024
    c_pad = jnp.pad(c, ((0, SUBLANES - bsz), (0, 0)))
    out = pl.pallas_call(
        _mod_kernel,
        grid=(n // bn,),
        in_specs=[
            pl.BlockSpec((SUBLANES, d), lambda j: (0, 0)),
            pl.BlockSpec((d, bn), lambda j: (0, j)),
            pl.BlockSpec((1, bn), lambda j: (0, j)),
        ],
        out_specs=pl.BlockSpec((SUBLANES, bn), lambda j: (0, j)),
        out_shape=jax.ShapeDtypeStruct((SUBLANES, n), F32),
        name="modulation",
    )(c_pad, w_ada, b_ada.reshape(1, n))
    return out[:bsz].reshape(bsz, N_MOD, d)


def _swiglu_residual(x, mod_ref, mod_base, gain_ref, w1_ref, w3_ref, w2_ref):
    shift = mod_ref[mod_base:mod_base + 1, :]
    scale = mod_ref[mod_base + 1:mod_base + 2, :]
    gate = mod_ref[mod_base + 2:mod_base + 3, :]
    h = _norm_mod(x, gain_ref[...], shift, scale).astype(BF16)
    a = _dot(h, w1_ref[...])
    b = _dot(h, w3_ref[...])
    g = (_silu(a) * b).astype(BF16)
    return x + (0.5 * gate) * _dot(g, w2_ref[...])


def _ffn_inproj_kernel(x_ref, mod_ref, gain_ref, w1_ref, w3_ref, w2_ref, mix_gain_ref, win_ref, x1_ref, z_ref):
    x1 = _swiglu_residual(x_ref[...], mod_ref, 0, gain_ref, w1_ref, w3_ref, w2_ref)
    x1_ref[...] = x1
    h = _norm_mod(x1, mix_gain_ref[...], mod_ref[3:4, :], mod_ref[4:5, :]).astype(BF16)
    z_ref[...] = _dot(h, win_ref[...])


def _resident(shape):
    return pl.BlockSpec(shape, lambda *_: (0,) * len(shape), pipeline_mode=pl.Buffered(1))


def _row_tiles(tm, width, col=0):
    return pl.BlockSpec((tm, width), lambda i: (i, col))


def _mod_rows(d, tiles_per_batch):
    return pl.BlockSpec((None, N_MOD, d), lambda i: (i // tiles_per_batch, 0, 0))


def _ffn_inproj(x2d, mod, gain, w1, w3, w2, mix_gain, w_in, *, tiles_per_batch):
    n, d = x2d.shape
    cols = w_in.shape[1]
    tm = ROW_TILE
    return pl.pallas_call(
        _ffn_inproj_kernel,
        grid=(n // tm,),
        in_specs=[_row_tiles(tm, d), _mod_rows(d, tiles_per_batch), _resident(gain.shape), _resident(w1.shape),
                  _resident(w3.shape), _resident(w2.shape), _resident(mix_gain.shape), _resident(w_in.shape)],
        out_specs=[_row_tiles(tm, d), _row_tiles(tm, cols)],
        out_shape=[jax.ShapeDtypeStruct((n, d), F32), jax.ShapeDtypeStruct((n, cols), F32)],
        compiler_params=pltpu.CompilerParams(
            dimension_semantics=("arbitrary",), vmem_limit_bytes=VMEM_LIMIT),
        name="ffn_inproj",
    )(x2d, mod, gain, w1, w3, w2, mix_gain, w_in)


def _lower_bound(lb_ref):
    p0 = lb_ref[0:1, :]
    p1 = lb_ref[1:2, :]
    m = jnp.maximum(p0, p1)
    e0 = jnp.exp(p0 - m)
    e1 = jnp.exp(p1 - m)
    return e1 / (e0 + e1)


def _split3(x):
    hi = x.astype(BF16)
    r = x - hi.astype(F32)
    mid = r.astype(BF16)
    lo = (r - mid.astype(F32)).astype(BF16)
    return hi, mid, lo


class _HgrnScratch(NamedTuple):
    q: object
    k: object
    lg3: object
    b: object
    v16: object
    qm: object
    km: object
    qi: object
    ks: object
    ds: object
    dec: object
    wide: object


def _chunk_rows(c):
    return slice(c * HG_CHUNK, (c + 1) * HG_CHUNK)


def _head_lanes(h):
    return slice(h * HG_HEAD_DIM, (h + 1) * HG_HEAD_DIM)


def _chunk_marks(reverse):
    c = HG_CHUNK
    return (c - 1, 0, c // 2) if reverse else (0, c - 1, c // 2 - 1)


def _hgrn_elementwise(q_ref, f_ref, v_ref, lb, sc):
    sc.q[...] = _silu(q_ref[...])
    f = lb + (1.0 - lb) * jax.nn.sigmoid(f_ref[...])
    sc.k[...] = 1.0 - f
    hi, mid, lo = _split3(jnp.log(f))
    sc.lg3[:, 0:D_HG] = hi
    sc.lg3[:, D_HG:2 * D_HG] = mid
    sc.lg3[:, 2 * D_HG:3 * D_HG] = lo
    sc.v16[...] = v_ref[...].astype(BF16)


def _hgrn_cumsum(sc, tri, n_chunks):
    for c in range(n_chunks):
        rows = _chunk_rows(c)
        p = _dot(tri, sc.lg3[rows, :])
        sc.b[rows, :] = p[:, 0:D_HG] + p[:, D_HG:2 * D_HG] + p[:, 2 * D_HG:3 * D_HG]


def _hgrn_factors(sc, n_chunks, *, reverse):
    first, last, ref = _chunk_marks(reverse)
    n_wide = None
    for c in range(n_chunks):
        rows = _chunk_rows(c)
        b2 = sc.b[rows, :] * LOG2_E
        r2 = b2[ref:ref + 1, :]
        end2 = b2[last:last + 1, :]
        spread = jnp.maximum(b2[first:first + 1, :] - r2, r2 - end2)
        wide = jnp.where(spread < HG_SAFE_EXPONENT_LOG2, 0.0, 1.0)
        sc.wide[c:c + 1, :] = wide
        n_wide = wide if n_wide is None else n_wide + wide
        narrow = jnp.broadcast_to(wide, b2.shape) == 0.0
        q, k = sc.q[rows, :], sc.k[rows, :]
        sc.qm[rows, :] = jnp.where(narrow, q * jnp.exp2(b2 - r2), 0.0).astype(BF16)
        sc.km[rows, :] = jnp.where(narrow, k * jnp.exp2(r2 - b2), 0.0).astype(BF16)
        sc.qi[rows, :] = (q * jnp.exp2(b2)).astype(BF16)
        sc.ks[rows, :] = (k * jnp.exp2(end2 - b2)).astype(BF16)
        sc.dec[c:c + 1, :] = jnp.exp2(end2)
    return n_wide


def _hgrn_state_increment(sc, c, h):
    rows, sl = _chunk_rows(c), _head_lanes(h)
    sc.ds[c, h] = _dot_tn(sc.v16[rows, sl], sc.ks[rows, sl])


def _hgrn_intra_factored(sc, o_ref, tri, n_chunks):
    causal = tri > 0
    pairs = [(c, h) for c in range(n_chunks) for h in range(HG_HEADS)]
    for c, h in pairs:
        _hgrn_state_increment(sc, c, h)
    scores = {}
    for c, h in pairs:
        rows, sl = _chunk_rows(c), _head_lanes(h)
        scores[c, h] = jnp.where(causal, _dot_nt(sc.qm[rows, sl], sc.km[rows, sl]), 0.0).astype(BF16)
    for c, h in pairs:
        rows, sl = _chunk_rows(c), _head_lanes(h)
        o_ref[rows, sl] = _dot(scores[c, h], sc.v16[rows, sl])


def _hgrn_intra_halving(sc, o_ref, n_chunks, *, reverse):
    cs = HG_CHUNK
    row = lax.broadcasted_iota(jnp.int32, (cs, cs), 0)
    col = lax.broadcasted_iota(jnp.int32, (cs, cs), 1)
    pos = lax.broadcasted_iota(jnp.int32, (cs, D_HG), 0)
    for c in range(n_chunks):
        rows = _chunk_rows(c)
        wide = jnp.broadcast_to(sc.wide[c:c + 1, :], (cs, D_HG)) > 0.0
        b, k = sc.b[rows, :], sc.k[rows, :]
        q = jnp.where(wide, sc.q[rows, :], 0.0)
        q16, k16 = q.astype(BF16), k.astype(BF16)
        scores = [jnp.where(row == col, _dot_nt(q16[:, _head_lanes(h)], k16[:, _head_lanes(h)]), 0.0)
                  for h in range(HG_HEADS)]
        b_hi, b_mid, b_lo = _split3(b)
        shift = 1
        while (1 << shift) <= cs:
            blk, half = 1 << shift, 1 << (shift - 1)
            boundary = ((row >> shift) << shift) + (half if reverse else half - 1)
            pick = jnp.where(col == boundary, 1.0, 0.0).astype(BF16)
            b_bnd = _dot(pick, b_hi) + _dot(pick, b_mid) + _dot(pick, b_lo)
            in_half = (pos & (blk - 1)) >= half
            attends = jnp.logical_not(in_half) if reverse else in_half
            qh = jnp.where(attends, q * jnp.exp(b - b_bnd), 0.0).astype(BF16)
            kh = jnp.where(attends, 0.0, k * jnp.exp(b_bnd - b)).astype(BF16)
            same_block = (row >> shift) == (col >> shift)
            for h in range(HG_HEADS):
                sl = _head_lanes(h)
                scores[h] = scores[h] + jnp.where(same_block, _dot_nt(qh[:, sl], kh[:, sl]), 0.0)
            shift += 1
        for h in range(HG_HEADS):
            sl = _head_lanes(h)
            o_ref[rows, sl] += _dot(scores[h].astype(BF16), sc.v16[rows, sl])


def _hgrn_carry_state(sc, o_ref, s_ref, n_chunks, *, reverse):
    order = range(n_chunks - 1, -1, -1) if reverse else range(n_chunks)
    for c in order:
        rows = _chunk_rows(c)
        for h in range(HG_HEADS):
            sl = _head_lanes(h)
            st = s_ref[h]
            o_ref[rows, sl] += _dot_nt(sc.qi[rows, sl], st.astype(BF16))
            s_ref[h] = st * sc.dec[c:c + 1, sl] + sc.ds[c, h]


def _hgrn_kernel(qf_ref, ff_ref, vf_ref, qb_ref, fb_ref, vb_ref, lbf_ref, lbb_ref, of_ref, ob_ref, s_ref, *scratch):
    tm = of_ref.shape[0]
    n_chunks = tm // HG_CHUNK
    sc_f = _HgrnScratch(*(r.at[0] for r in scratch))
    sc_b = _HgrnScratch(*(r.at[1] for r in scratch))

    @pl.when(pl.program_id(1) == 0)
    def _():
        s_ref[...] = jnp.zeros_like(s_ref)

    row = lax.broadcasted_iota(jnp.int32, (HG_CHUNK, HG_CHUNK), 0)
    col = lax.broadcasted_iota(jnp.int32, (HG_CHUNK, HG_CHUNK), 1)
    tri_f = jnp.where(col <= row, 1.0, 0.0).astype(BF16)
    tri_b = jnp.where(col >= row, 1.0, 0.0).astype(BF16)

    _hgrn_elementwise(qf_ref, ff_ref, vf_ref, _lower_bound(lbf_ref), sc_f)
    _hgrn_elementwise(qb_ref, fb_ref, vb_ref, _lower_bound(lbb_ref), sc_b)
    _hgrn_cumsum(sc_f, tri_f, n_chunks)
    _hgrn_cumsum(sc_b, tri_b, n_chunks)
    n_wide = _hgrn_factors(sc_f, n_chunks, reverse=False) + _hgrn_factors(sc_b, n_chunks, reverse=True)
    _hgrn_intra_factored(sc_f, of_ref, tri_f, n_chunks)
    _hgrn_intra_factored(sc_b, ob_ref, tri_b, n_chunks)
    _hgrn_carry_state(sc_f, of_ref, s_ref.at[0], n_chunks, reverse=False)
    _hgrn_carry_state(sc_b, ob_ref, s_ref.at[1], n_chunks, reverse=True)

    @pl.when(jnp.max(n_wide) > 0.0)
    def _():
        _hgrn_intra_halving(sc_f, of_ref, n_chunks, reverse=False)
        _hgrn_intra_halving(sc_b, ob_ref, n_chunks, reverse=True)


def _hgrn_scan(z3d, lb_fwd, lb_bwd):
    bsz, t, _ = z3d.shape
    tm = HGRN_TILE
    n_t = t // tm
    n_chunks = tm // HG_CHUNK

    def fwd(col):
        return pl.BlockSpec((None, tm, D_HG), lambda b, j: (b, j, col))

    def bwd(col):
        return pl.BlockSpec((None, tm, D_HG), lambda b, j: (b, n_t - 1 - j, col))

    lb_spec = pl.BlockSpec(lb_fwd.shape, lambda b, j: (0, 0))
    tile = lambda width, dtype: pltpu.VMEM((2, tm, width), dtype)
    scratch = _HgrnScratch(
        q=tile(D_HG, F32), k=tile(D_HG, F32), lg3=tile(3 * D_HG, BF16), b=tile(D_HG, F32), v16=tile(D_HG, BF16),
        qm=tile(D_HG, BF16), km=tile(D_HG, BF16), qi=tile(D_HG, BF16), ks=tile(D_HG, BF16),
        ds=pltpu.VMEM((2, n_chunks, HG_HEADS, HG_HEAD_DIM, HG_HEAD_DIM), F32),
        dec=pltpu.VMEM((2, n_chunks, D_HG), F32), wide=pltpu.VMEM((2, n_chunks, D_HG), F32))
    return pl.pallas_call(
        _hgrn_kernel,
        grid=(bsz, n_t),
        in_specs=[fwd(COL_Q), fwd(COL_F_FWD), fwd(COL_I), bwd(COL_Q), bwd(COL_F_BWD), bwd(COL_I),
                  lb_spec, lb_spec],
        out_specs=[pl.BlockSpec((None, tm, D_HG), lambda b, j: (b, j, 0)),
                   pl.BlockSpec((None, tm, D_HG), lambda b, j: (b, n_t - 1 - j, 0))],
        out_shape=[jax.ShapeDtypeStruct((bsz, t, D_HG), F32)] * 2,
        scratch_shapes=[pltpu.VMEM((2, HG_HEADS, HG_HEAD_DIM, HG_HEAD_DIM), F32), *scratch],
        compiler_params=pltpu.CompilerParams(
            dimension_semantics=("arbitrary", "arbitrary"), vmem_limit_bytes=VMEM_LIMIT),
        name="hgrn_scan",
    )(z3d, z3d, z3d, z3d, z3d, z3d, lb_fwd, lb_bwd)


def _softplus(x):
    return jnp.maximum(x, 0.0) + jnp.log1p(jnp.exp(-jnp.abs(x)))


def _lane_groups(width):
    return [slice(g * LANES, (g + 1) * LANES) for g in range(width // LANES)]


def _segment_pitch(seg_len):
    assert seg_len % (2 * SUBLANES) == 0
    return seg_len + SUBLANES


def _segment_rows(stage_ref, i, seg_len):
    rows = pl.ds(i, SUBLANES, stride=_segment_pitch(seg_len))
    return jnp.concatenate([stage_ref[g, rows, :] for g in range(stage_ref.shape[0])], axis=1)


def _stage_segments(stage_ref, x_ref, seg_len):
    pitch = _segment_pitch(seg_len)
    for g, lanes in enumerate(_lane_groups(x_ref.shape[1])):
        for s in range(SUBLANES):
            stage_ref[g, s * pitch:s * pitch + seg_len, :] = x_ref[s * seg_len:(s + 1) * seg_len, lanes]


def _unstage_segments(h_ref, stage_ref, seg_len):
    pitch = _segment_pitch(seg_len)
    for g, lanes in enumerate(_lane_groups(h_ref.shape[1])):
        for s in range(SUBLANES):
            h_ref[s * seg_len:(s + 1) * seg_len, lanes] = stage_ref[g, s * pitch:s * pitch + seg_len, :]


def _shift_segments(v, fill_row, *, down):
    sub = lax.broadcasted_iota(jnp.int32, v.shape, 0)
    if down:
        return jnp.where(sub >= 1, pltpu.roll(v, 1, 0), fill_row)
    return jnp.where(sub < SUBLANES - 1, pltpu.roll(v, SUBLANES - 1, 0), fill_row)


class _SegmentedScan:
    def __init__(self, n, reverse):
        self.n, self.reverse = n, reverse
        self.h, self.a_cum, self.last = [None] * n, [None] * n, None

    def order(self):
        return list(range(self.n - 1, -1, -1) if self.reverse else range(self.n))

    def feed(self, i, a, u):
        if self.last is None:
            self.h[i], self.a_cum[i] = u, a
        else:
            self.h[i], self.a_cum[i] = a * self.h[self.last] + u, a * self.a_cum[self.last]
        self.last = i

    def finish(self, carry):
        h_end, a_end = self.h[self.last], self.a_cum[self.last]
        entering = [None] * SUBLANES
        for s in (range(SUBLANES - 1, -1, -1) if self.reverse else range(SUBLANES)):
            entering[s] = carry
            carry = h_end[s:s + 1, :] + a_end[s:s + 1, :] * carry
        entering = jnp.concatenate(entering, axis=0)
        return [self.h[i] + self.a_cum[i] * entering for i in range(self.n)], carry


def _rglru_conv_gates(x_ref, prev_ref, next_ref, cw_ref, cb_ref, wa_ref, wx_ref, stage_ref, first_tile, last_tile):
    tm = x_ref.shape[0]
    n = tm // SUBLANES
    _stage_segments(stage_ref, x_ref, n)
    xs = [_segment_rows(stage_ref, i, n) for i in range(n)]
    prev = jnp.where(first_tile, 0.0, prev_ref[...])
    nxt = jnp.where(last_tile, 0.0, next_ref[...])
    edge = {-2: _shift_segments(xs[n - 2], prev[HALO - 2:HALO - 1, :], down=True),
            -1: _shift_segments(xs[n - 1], prev[HALO - 1:HALO, :], down=True),
            n: _shift_segments(xs[0], nxt[0:1, :], down=False)}
    tap = lambda i: xs[i] if 0 <= i < n else edge[i]
    cw = [cw_ref[w:w + 1, :] for w in range(RG_CONV_W)]
    cb = cb_ref[...]
    xc = []
    for i in range(n):
        acc = cb + tap(i - RG_CONV_LEFT) * cw[0]
        for w in range(1, RG_CONV_W):
            acc = acc + tap(i - RG_CONV_LEFT + w) * cw[w]
        xc.append(acc)
    xc = jnp.concatenate(xc, axis=0)
    xc16 = xc.astype(BF16)
    return xc, _dot(xc16, wa_ref[...]), _dot(xc16, wx_ref[...])


def _rglru_recurrence(xc, pre_r, pre_i, ba_ref, bx_ref, lam_ref, h_ref, stage_ref, carry_ref, *, reverse):
    n = xc.shape[0] // SUBLANES
    r = jax.nn.sigmoid(pre_r + ba_ref[...])
    gate_i = jax.nn.sigmoid(pre_i + bx_ref[...])
    log_a = (-RG_C) * r * _softplus(-lam_ref[...])
    a = jnp.exp(log_a)
    y = -jnp.tanh(log_a) * (a * a + 1.0)
    u = jnp.where(y > 0.0, y * lax.rsqrt(y), 0.0) * (gate_i * xc)
    scan = _SegmentedScan(n, reverse)
    for i in scan.order():
        vec = slice(i * SUBLANES, (i + 1) * SUBLANES)
        scan.feed(i, a[vec, :], u[vec, :])
    h, carry = scan.finish(carry_ref[...])
    carry_ref[...] = carry
    for i in range(n):
        for g, lanes in enumerate(_lane_groups(D_RG)):
            stage_ref[g, pl.ds(i, SUBLANES, stride=_segment_pitch(n)), :] = h[i][:, lanes]
    _unstage_segments(h_ref, stage_ref, n)


def _rglru_direction(x_ref, prev_ref, next_ref, cw_ref, cb_ref, wa_ref, ba_ref, wx_ref, bx_ref,
                     lam_ref, h_ref, stage_ref, carry_ref, first_tile, last_tile, *, reverse):
    pre = _rglru_conv_gates(x_ref, prev_ref, next_ref, cw_ref, cb_ref, wa_ref, wx_ref, stage_ref,
                            first_tile, last_tile)
    _rglru_recurrence(*pre, ba_ref, bx_ref, lam_ref, h_ref, stage_ref, carry_ref, reverse=reverse)


def _rglru_fwd_kernel(x_ref, prev_ref, next_ref, cw_ref, cb_ref, wa_ref, ba_ref, wx_ref, bx_ref, lam_ref,
                      h_ref, stage_ref, carry_ref):
    j = pl.program_id(1)

    @pl.when(j == 0)
    def _():
        carry_ref[...] = jnp.zeros_like(carry_ref)

    _rglru_direction(x_ref, prev_ref, next_ref, cw_ref, cb_ref, wa_ref, ba_ref, wx_ref, bx_ref, lam_ref,
                     h_ref, stage_ref, carry_ref, j == 0, j == pl.num_programs(1) - 1, reverse=False)


def _rx_tile_specs(tm, t, tile_of):
    halo_per_tile = tm // HALO
    n_halo = t // HALO
    main = pl.BlockSpec((None, tm, D_RG), lambda b, j: (b, tile_of(j), COL_RX))
    prev = pl.BlockSpec(
        (None, HALO, D_RG), lambda b, j: (b, jnp.maximum(tile_of(j) * halo_per_tile - 1, 0), COL_RX))
    nxt = pl.BlockSpec(
        (None, HALO, D_RG), lambda b, j: (b, jnp.minimum((tile_of(j) + 1) * halo_per_tile, n_halo - 1), COL_RX))
    return [main, prev, nxt]


def _rglru_scratch(tm):
    return [pltpu.VMEM((D_RG // LANES, SUBLANES * _segment_pitch(tm // SUBLANES), LANES), F32),
            pltpu.VMEM((1, D_RG), F32)]


def _whole(a):
    return pl.BlockSpec(a.shape, lambda *_: (0,) * a.ndim)


def _rglru_fwd_scan(z3d, conv_w, conv_b, gates):
    bsz, t, _ = z3d.shape
    tm = RGLRU_FWD_TILE
    params = [conv_w, conv_b, *gates]
    return pl.pallas_call(
        _rglru_fwd_kernel,
        grid=(bsz, t // tm),
        in_specs=_rx_tile_specs(tm, t, lambda j: j) + [_whole(p) for p in params],
        out_specs=pl.BlockSpec((None, tm, D_RG), lambda b, j: (b, j, 0)),
        out_shape=jax.ShapeDtypeStruct((bsz, t, D_RG), F32),
        scratch_shapes=_rglru_scratch(tm),
        compiler_params=pltpu.CompilerParams(
            dimension_semantics=("arbitrary", "arbitrary"), vmem_limit_bytes=VMEM_LIMIT),
        name="rglru_fwd_scan",
    )(z3d, z3d, z3d, *params)


def _gelu_tanh(x):
    return 0.5 * x * (1.0 + jnp.tanh(0.7978845608028654 * (x + 0.044715 * (x * x * x))))


def _mix_residual(x, gate, of_ref, ob_ref, g_ref, hf_ref, hb_ref, rgate_ref, hgain_ref, w_ref):
    o = of_ref[...] + ob_ref[...]
    g = _silu(g_ref[...])
    o_rg = ((hf_ref[...] + hb_ref[...]) * _gelu_tanh(rgate_ref[...])).astype(BF16)
    y = _dot(o_rg, w_ref[D_HG:, :])
    for h in range(HG_HEADS):
        sl = _head_lanes(h)
        o_hg = (_rms(o[:, sl], hgain_ref[...]) * g[:, sl]).astype(BF16)
        y = y + _dot(o_hg, w_ref[sl, :])
    return x + gate * y


def _mixout_kernel(x_ref, mod_ref, of_ref, ob_ref, g_ref, hf_ref, rgate_ref, rx_ref, rxprev_ref, rxnext_ref,
                   cw_ref, cb_ref, wa_ref, ba_ref, wx_ref, bx_ref, lam_ref, hgain_ref, wout_ref,
                   gain_ref, w1_ref, w3_ref, w2_ref, fgain_ref, o_ref, hb_ref, stage_ref, carry_ref):
    s = pl.program_id(1)
    n_t = pl.num_programs(1) - 1
    scan_tile = n_t - 1 - s

    def scan():
        _rglru_direction(rx_ref, rxprev_ref, rxnext_ref, cw_ref, cb_ref, wa_ref, ba_ref, wx_ref, bx_ref, lam_ref,
                         hb_ref, stage_ref, carry_ref, scan_tile == 0, scan_tile == n_t - 1, reverse=True)

    def mix_ffn():
        x2 = _mix_residual(x_ref[...], mod_ref[5:6, :], of_ref, ob_ref, g_ref, hf_ref, hb_ref,
                           rgate_ref, hgain_ref, wout_ref)
        x3 = _swiglu_residual(x2, mod_ref, 6, gain_ref, w1_ref, w3_ref, w2_ref)
        o_ref[...] = _rms(x3, fgain_ref[...])

    @pl.when(s == 0)
    def _():
        carry_ref[...] = jnp.zeros_like(carry_ref)
        scan()

    @pl.when(jnp.logical_and(s > 0, s < n_t))
    def _():
        mix_ffn()
        scan()

    @pl.when(s == n_t)
    def _():
        mix_ffn()


def _mixout(x3d, mod, o_f, o_b, z3d, h_f, conv_w, conv_b, gates_bwd, hg_gain, w_out, gain, w1, w3, w2, final_gain):
    bsz, t, d = x3d.shape
    tm = ROW_TILE
    n_t = t // tm
    ffn_tile = lambda s: jnp.minimum(n_t - s, n_t - 1)
    scan_tile = lambda s: jnp.maximum(n_t - 1 - s, 0)

    def rows(width, col=0):
        return pl.BlockSpec((None, tm, width), lambda b, s: (b, ffn_tile(s), col))

    scan_params = [conv_w, conv_b, *gates_bwd]
    ffn_params = [hg_gain, w_out, gain, w1, w3, w2, final_gain]
    return pl.pallas_call(
        _mixout_kernel,
        grid=(bsz, n_t + 1),
        in_specs=[rows(d), pl.BlockSpec((None, N_MOD, d), lambda b, s: (b, 0, 0)),
                  rows(D_HG), rows(D_HG), rows(D_HG, COL_G), rows(D_RG), rows(D_RG, COL_RGATE),
                  *_rx_tile_specs(tm, t, scan_tile),
                  *[_whole(p) for p in scan_params], *[_resident(p.shape) for p in ffn_params]],
        out_specs=rows(d),
        out_shape=jax.ShapeDtypeStruct((bsz, t, d), F32),
        scratch_shapes=[pltpu.VMEM((tm, D_RG), F32), *_rglru_scratch(tm)],
        compiler_params=pltpu.CompilerParams(
            dimension_semantics=("arbitrary", "arbitrary"), vmem_limit_bytes=VMEM_LIMIT),
        name="mixout",
    )(x3d, mod, o_f, o_b, z3d, h_f, z3d, z3d, z3d, z3d, *scan_params, *ffn_params)


def _block_diag(w):
    nb, d, e = w.shape
    eye = jnp.eye(nb, dtype=w.dtype)
    return (eye[:, None, :, None] * w[:, :, None, :]).reshape(nb * d, nb * e)


def kernel(x, c, w_ada, b_ada, ffn1_norm, ffn1_w1, ffn1_w3, ffn1_w2, mix_norm, w_in, hg_lb_fwd, hg_lb_bwd, hg_out_norm, rg_conv_w, rg_conv_b, rg_wa_fwd, rg_ba_fwd, rg_wx_fwd, rg_bx_fwd, rg_lam_fwd, rg_wa_bwd, rg_ba_bwd, rg_wx_bwd, rg_bx_bwd, rg_lam_bwd, w_out, ffn2_norm, ffn2_w1, ffn2_w3, ffn2_w2, final_norm):
    bsz, t, d = x.shape
    depth = w_ada.shape[0]
    assert depth == 1 and hg_lb_fwd.shape[0] == 2, "kernels are written for a single layer"
    assert t % ROW_TILE == 0 and t % RGLRU_FWD_TILE == 0 and t % HGRN_TILE == 0 and HGRN_TILE % HG_CHUNK == 0
    tiles_per_batch = t // ROW_TILE
    row = lambda v: v.reshape(1, -1)
    l = 0

    mod = _modulation(c, w_ada[l], b_ada[l])
    bf16 = lambda w: w[l].astype(BF16)
    flat = lambda a: a.reshape(bsz * t, a.shape[-1])

    x1, z2d = _ffn_inproj(flat(x), mod, row(ffn1_norm[l]), bf16(ffn1_w1), bf16(ffn1_w3), bf16(ffn1_w2),
                          row(mix_norm[l]), bf16(w_in), tiles_per_batch=tiles_per_batch)
    z3d = z2d.reshape(bsz, t, IN_COLS)
    o_f, o_b = _hgrn_scan(z3d, hg_lb_fwd, hg_lb_bwd)

    def gates(wa, ba, wx, bx, lam):
        return (_block_diag(wa[l]).astype(BF16), row(ba[l]), _block_diag(wx[l]).astype(BF16), row(bx[l]),
                row(lam[l]))

    conv_w, conv_b = rg_conv_w[l], row(rg_conv_b[l])
    h_f = _rglru_fwd_scan(z3d, conv_w, conv_b, gates(rg_wa_fwd, rg_ba_fwd, rg_wx_fwd, rg_bx_fwd, rg_lam_fwd))

    return _mixout(x1.reshape(bsz, t, d), mod, o_f, o_b, z3d, h_f, conv_w, conv_b,
                   gates(rg_wa_bwd, rg_ba_bwd, rg_wx_bwd, rg_bx_bwd, rg_lam_bwd), row(hg_out_norm[l]), bf16(w_out),
                   row(ffn2_norm[l]), bf16(ffn2_w1), bf16(ffn2_w3), bf16(ffn2_w2), row(final_norm))
```

```python
import math
from typing import NamedTuple

import jax
import jax.numpy as jnp
from jax import lax
from jax.experimental import pallas as pl
from jax.experimental.pallas import tpu as pltpu

F32 = jnp.float32
BF16 = jnp.bfloat16

EPS = 1e-6
N_MOD = 9
HG_HEAD_DIM = 128
HG_HEADS = 4
D_HG = HG_HEADS * HG_HEAD_DIM
RG_BLOCKS = 8
D_RG = 512
RG_CONV_W = 4
RG_CONV_LEFT = RG_CONV_W // 2
RG_C = 8.0
IN_COLS = 5 * D_HG + 2 * D_RG
COL_Q, COL_F_FWD, COL_F_BWD, COL_I, COL_G, COL_RX, COL_RGATE = range(7)

SUBLANES = 8
LANES = 128
HALO = SUBLANES
HG_CHUNK = 64
LOG2_E = math.log2(math.e)
HG_SAFE_EXPONENT_LOG2 = 115.0
VMEM_LIMIT = 56 * 1024 * 1024

ROW_TILE = 256
RGLRU_FWD_TILE = 1024
HGRN_TILE = 512


def _rms(x, gain):
    return x * lax.rsqrt(jnp.mean(x * x, axis=-1, keepdims=True) + EPS) * gain


def _norm_mod(x, gain, shift, scale):
    return _rms(x, gain) * (1.0 + scale) + shift


def _silu(x):
    return x * jax.nn.sigmoid(x)


def _dot(a, b):
    return jnp.dot(a, b, preferred_element_type=F32)


def _dot_nt(a, b):
    return lax.dot_general(a, b, (((1,), (1,)), ((), ())), preferred_element_type=F32)


def _dot_tn(a, b):
    return lax.dot_general(a, b, (((0,), (0,)), ((), ())), preferred_element_type=F32)


def _mod_kernel(c_ref, w_ref, b_ref, o_ref):
    o_ref[...] = _dot(_silu(c_ref[...]), w_ref[...]) + b_ref[...]


def _modulation(c, w_ada, b_ada):
    bsz, d = c.shape
    n = w_ada.shape[1]
    bn = 1024
    c_pad = jnp.pad(c, ((0, SUBLANES - bsz), (0, 0)))
    out = pl.pallas_call(
        _mod_kernel,
        grid=(n // bn,),
        in_specs=[
            pl.BlockSpec((SUBLANES, d), lambda j: (0, 0)),
            pl.BlockSpec((d, bn), lambda j: (0, j)),
            pl.BlockSpec((1, bn), lambda j: (0, j)),
        ],
        out_specs=pl.BlockSpec((SUBLANES, bn), lambda j: (0, j)),
        out_shape=jax.ShapeDtypeStruct((SUBLANES, n), F32),
        name="modulation",
    )(c_pad, w_ada, b_ada.reshape(1, n))
    return out[:bsz].reshape(bsz, N_MOD, d)


def _swiglu_residual(x, mod_ref, mod_base, gain_ref, w1_ref, w3_ref, w2_ref):
    shift = mod_ref[mod_base:mod_base + 1, :]
    scale = mod_ref[mod_base + 1:mod_base + 2, :]
    gate = mod_ref[mod_base + 2:mod_base + 3, :]
    h = _norm_mod(x, gain_ref[...], shift, scale).astype(BF16)
    a = _dot(h, w1_ref[...])
    b = _dot(h, w3_ref[...])
    g = (_silu(a) * b).astype(BF16)
    return x + (0.5 * gate) * _dot(g, w2_ref[...])


def _ffn_inproj_kernel(x_ref, mod_ref, gain_ref, w1_ref, w3_ref, w2_ref, mix_gain_ref, win_ref, x1_ref, z_ref):
    x1 = _swiglu_residual(x_ref[...], mod_ref, 0, gain_ref, w1_ref, w3_ref, w2_ref)
    x1_ref[...] = x1
    h = _norm_mod(x1, mix_gain_ref[...], mod_ref[3:4, :], mod_ref[4:5, :]).astype(BF16)
    z_ref[...] = _dot(h, win_ref[...])


def _resident(shape):
    return pl.BlockSpec(shape, lambda *_: (0,) * len(shape), pipeline_mode=pl.Buffered(1))


def _row_tiles(tm, width, col=0):
    return pl.BlockSpec((tm, width), lambda i: (i, col))


def _mod_rows(d, tiles_per_batch):
    return pl.BlockSpec((None, N_MOD, d), lambda i: (i // tiles_per_batch, 0, 0))


def _ffn_inproj(x2d, mod, gain, w1, w3, w2, mix_gain, w_in, *, tiles_per_batch):
    n, d = x2d.shape
    cols = w_in.shape[1]
    tm = ROW_TILE
    return pl.pallas_call(
        _ffn_inproj_kernel,
        grid=(n // tm,),
        in_specs=[_row_tiles(tm, d), _mod_rows(d, tiles_per_batch), _resident(gain.shape), _resident(w1.shape),
                  _resident(w3.shape), _resident(w2.shape), _resident(mix_gain.shape), _resident(w_in.shape)],
        out_specs=[_row_tiles(tm, d), _row_tiles(tm, cols)],
        out_shape=[jax.ShapeDtypeStruct((n, d), F32), jax.ShapeDtypeStruct((n, cols), F32)],
        compiler_params=pltpu.CompilerParams(
            dimension_semantics=("arbitrary",), vmem_limit_bytes=VMEM_LIMIT),
        name="ffn_inproj",
    )(x2d, mod, gain, w1, w3, w2, mix_gain, w_in)


def _lower_bound(lb_ref):
    p0 = lb_ref[0:1, :]
    p1 = lb_ref[1:2, :]
    m = jnp.maximum(p0, p1)
    e0 = jnp.exp(p0 - m)
    e1 = jnp.exp(p1 - m)
    return e1 / (e0 + e1)


def _split3(x):
    hi = x.astype(BF16)
    r = x - hi.astype(F32)
    mid = r.astype(BF16)
    lo = (r - mid.astype(F32)).astype(BF16)
    return hi, mid, lo


class _HgrnScratch(NamedTuple):
    q: object
    k: object
    lg3: object
    b: object
    v16: object
    qm: object
    km: object
    qi: object
    ks: object
    ds: object
    dec: object
    wide: object


def _chunk_rows(c):
    return slice(c * HG_CHUNK, (c + 1) * HG_CHUNK)


def _head_lanes(h):
    return slice(h * HG_HEAD_DIM, (h + 1) * HG_HEAD_DIM)


def _chunk_marks(reverse):
    c = HG_CHUNK
    return (c - 1, 0, c // 2) if reverse else (0, c - 1, c // 2 - 1)


def _hgrn_elementwise(q_ref, f_ref, v_ref, lb, sc):
    sc.q[...] = _silu(q_ref[...])
    f = lb + (1.0 - lb) * jax.nn.sigmoid(f_ref[...])
    sc.k[...] = 1.0 - f
    hi, mid, lo = _split3(jnp.log(f))
    sc.lg3[:, 0:D_HG] = hi
    sc.lg3[:, D_HG:2 * D_HG] = mid
    sc.lg3[:, 2 * D_HG:3 * D_HG] = lo
    sc.v16[...] = v_ref[...].astype(BF16)


def _hgrn_cumsum(sc, tri, n_chunks):
    for c in range(n_chunks):
        rows = _chunk_rows(c)
        p = _dot(tri, sc.lg3[rows, :])
        sc.b[rows, :] = p[:, 0:D_HG] + p[:, D_HG:2 * D_HG] + p[:, 2 * D_HG:3 * D_HG]


def _hgrn_factors(sc, n_chunks, *, reverse):
    first, last, ref = _chunk_marks(reverse)
    n_wide = None
    for c in range(n_chunks):
        rows = _chunk_rows(c)
        b2 = sc.b[rows, :] * LOG2_E
        r2 = b2[ref:ref + 1, :]
        end2 = b2[last:last + 1, :]
        spread = jnp.maximum(b2[first:first + 1, :] - r2, r2 - end2)
        wide = jnp.where(spread < HG_SAFE_EXPONENT_LOG2, 0.0, 1.0)
        sc.wide[c:c + 1, :] = wide
        n_wide = wide if n_wide is None else n_wide + wide
        narrow = jnp.broadcast_to(wide, b2.shape) == 0.0
        q, k = sc.q[rows, :], sc.k[rows, :]
        sc.qm[rows, :] = jnp.where(narrow, q * jnp.exp2(b2 - r2), 0.0).astype(BF16)
        sc.km[rows, :] = jnp.where(narrow, k * jnp.exp2(r2 - b2), 0.0).astype(BF16)
        sc.qi[rows, :] = (q * jnp.exp2(b2)).astype(BF16)
        sc.ks[rows, :] = (k * jnp.exp2(end2 - b2)).astype(BF16)
        sc.dec[c:c + 1, :] = jnp.exp2(end2)
    return n_wide


def _hgrn_state_increment(sc, c, h):
    rows, sl = _chunk_rows(c), _head_lanes(h)
    sc.ds[c, h] = _dot_tn(sc.v16[rows, sl], sc.ks[rows, sl])


def _hgrn_intra_factored(sc, o_ref, tri, n_chunks):
    causal = tri > 0
    pairs = [(c, h) for c in range(n_chunks) for h in range(HG_HEADS)]
    for c, h in pairs:
        _hgrn_state_increment(sc, c, h)
    scores = {}
    for c, h in pairs:
        rows, sl = _chunk_rows(c), _head_lanes(h)
        scores[c, h] = jnp.where(causal, _dot_nt(sc.qm[rows, sl], sc.km[rows, sl]), 0.0).astype(BF16)
    for c, h in pairs:
        rows, sl = _chunk_rows(c), _head_lanes(h)
        o_ref[rows, sl] = _dot(scores[c, h], sc.v16[rows, sl])


def _hgrn_intra_halving(sc, o_ref, n_chunks, *, reverse):
    cs = HG_CHUNK
    row = lax.broadcasted_iota(jnp.int32, (cs, cs), 0)
    col = lax.broadcasted_iota(jnp.int32, (cs, cs), 1)
    pos = lax.broadcasted_iota(jnp.int32, (cs, D_HG), 0)
    for c in range(n_chunks):
        rows = _chunk_rows(c)
        wide = jnp.broadcast_to(sc.wide[c:c + 1, :], (cs, D_HG)) > 0.0
        b, k = sc.b[rows, :], sc.k[rows, :]
        q = jnp.where(wide, sc.q[rows, :], 0.0)
        q16, k16 = q.astype(BF16), k.astype(BF16)
        scores = [jnp.where(row == col, _dot_nt(q16[:, _head_lanes(h)], k16[:, _head_lanes(h)]), 0.0)
                  for h in range(HG_HEADS)]
        b_hi, b_mid, b_lo = _split3(b)
        shift = 1
        while (1 << shift) <= cs:
            blk, half = 1 << shift, 1 << (shift - 1)
            boundary = ((row >> shift) << shift) + (half if reverse else half - 1)
            pick = jnp.where(col == boundary, 1.0, 0.0).astype(BF16)
            b_bnd = _dot(pick, b_hi) + _dot(pick, b_mid) + _dot(pick, b_lo)
            in_half = (pos & (blk - 1)) >= half
            attends = jnp.logical_not(in_half) if reverse else in_half
            qh = jnp.where(attends, q * jnp.exp(b - b_bnd), 0.0).astype(BF16)
            kh = jnp.where(attends, 0.0, k * jnp.exp(b_bnd - b)).astype(BF16)
            same_block = (row >> shift) == (col >> shift)
            for h in range(HG_HEADS):
                sl = _head_lanes(h)
                scores[h] = scores[h] + jnp.where(same_block, _dot_nt(qh[:, sl], kh[:, sl]), 0.0)
            shift += 1
        for h in range(HG_HEADS):
            sl = _head_lanes(h)
            o_ref[rows, sl] += _dot(scores[h].astype(BF16), sc.v16[rows, sl])


def _hgrn_carry_state(sc, o_ref, s_ref, n_chunks, *, reverse):
    order = range(n_chunks - 1, -1, -1) if reverse else range(n_chunks)
    for c in order:
        rows = _chunk_rows(c)
        for h in range(HG_HEADS):
            sl = _head_lanes(h)
            st = s_ref[h]
            o_ref[rows, sl] += _dot_nt(sc.qi[rows, sl], st.astype(BF16))
            s_ref[h] = st * sc.dec[c:c + 1, sl] + sc.ds[c, h]


def _hgrn_kernel(qf_ref, ff_ref, vf_ref, qb_ref, fb_ref, vb_ref, lbf_ref, lbb_ref, of_ref, ob_ref, s_ref, *scratch):
    tm = of_ref.shape[0]
    n_chunks = tm // HG_CHUNK
    sc_f = _HgrnScratch(*(r.at[0] for r in scratch))
    sc_b = _HgrnScratch(*(r.at[1] for r in scratch))

    @pl.when(pl.program_id(1) == 0)
    def _():
        s_ref[...] = jnp.zeros_like(s_ref)

    row = lax.broadcasted_iota(jnp.int32, (HG_CHUNK, HG_CHUNK), 0)
    col = lax.broadcasted_iota(jnp.int32, (HG_CHUNK, HG_CHUNK), 1)
    tri_f = jnp.where(col <= row, 1.0, 0.0).astype(BF16)
    tri_b = jnp.where(col >= row, 1.0, 0.0).astype(BF16)

    _hgrn_elementwise(qf_ref, ff_ref, vf_ref, _lower_bound(lbf_ref), sc_f)
    _hgrn_elementwise(qb_ref, fb_ref, vb_ref, _lower_bound(lbb_ref), sc_b)
    _hgrn_cumsum(sc_f, tri_f, n_chunks)
    _hgrn_cumsum(sc_b, tri_b, n_chunks)
    n_wide = _hgrn_factors(sc_f, n_chunks, reverse=False) + _hgrn_factors(sc_b, n_chunks, reverse=True)
    _hgrn_intra_factored(sc_f, of_ref, tri_f, n_chunks)
    _hgrn_intra_factored(sc_b, ob_ref, tri_b, n_chunks)
    _hgrn_carry_state(sc_f, of_ref, s_ref.at[0], n_chunks, reverse=False)
    _hgrn_carry_state(sc_b, ob_ref, s_ref.at[1], n_chunks, reverse=True)

    @pl.when(jnp.max(n_wide) > 0.0)
    def _():
        _hgrn_intra_halving(sc_f, of_ref, n_chunks, reverse=False)
        _hgrn_intra_halving(sc_b, ob_ref, n_chunks, reverse=True)


def _hgrn_scan(z3d, lb_fwd, lb_bwd):
    bsz, t, _ = z3d.shape
    tm = HGRN_TILE
    n_t = t // tm
    n_chunks = tm // HG_CHUNK

    def fwd(col):
        return pl.BlockSpec((None, tm, D_HG), lambda b, j: (b, j, col))

    def bwd(col):
        return pl.BlockSpec((None, tm, D_HG), lambda b, j: (b, n_t - 1 - j, col))

    lb_spec = pl.BlockSpec(lb_fwd.shape, lambda b, j: (0, 0))
    tile = lambda width, dtype: pltpu.VMEM((2, tm, width), dtype)
    scratch = _HgrnScratch(
        q=tile(D_HG, F32), k=tile(D_HG, F32), lg3=tile(3 * D_HG, BF16), b=tile(D_HG, F32), v16=tile(D_HG, BF16),
        qm=tile(D_HG, BF16), km=tile(D_HG, BF16), qi=tile(D_HG, BF16), ks=tile(D_HG, BF16),
        ds=pltpu.VMEM((2, n_chunks, HG_HEADS, HG_HEAD_DIM, HG_HEAD_DIM), F32),
        dec=pltpu.VMEM((2, n_chunks, D_HG), F32), wide=pltpu.VMEM((2, n_chunks, D_HG), F32))
    return pl.pallas_call(
        _hgrn_kernel,
        grid=(bsz, n_t),
        in_specs=[fwd(COL_Q), fwd(COL_F_FWD), fwd(COL_I), bwd(COL_Q), bwd(COL_F_BWD), bwd(COL_I),
                  lb_spec, lb_spec],
        out_specs=[pl.BlockSpec((None, tm, D_HG), lambda b, j: (b, j, 0)),
                   pl.BlockSpec((None, tm, D_HG), lambda b, j: (b, n_t - 1 - j, 0))],
        out_shape=[jax.ShapeDtypeStruct((bsz, t, D_HG), F32)] * 2,
        scratch_shapes=[pltpu.VMEM((2, HG_HEADS, HG_HEAD_DIM, HG_HEAD_DIM), F32), *scratch],
        compiler_params=pltpu.CompilerParams(
            dimension_semantics=("arbitrary", "arbitrary"), vmem_limit_bytes=VMEM_LIMIT),
        name="hgrn_scan",
    )(z3d, z3d, z3d, z3d, z3d, z3d, lb_fwd, lb_bwd)


def _softplus(x):
    return jnp.maximum(x, 0.0) + jnp.log1p(jnp.exp(-jnp.abs(x)))


def _lane_groups(width):
    return [slice(g * LANES, (g + 1) * LANES) for g in range(width // LANES)]


def _segment_pitch(seg_len):
    assert seg_len % (2 * SUBLANES) == 0
    return seg_len + SUBLANES


def _segment_rows(stage_ref, i, seg_len):
    rows = pl.ds(i, SUBLANES, stride=_segment_pitch(seg_len))
    return jnp.concatenate([stage_ref[g, rows, :] for g in range(stage_ref.shape[0])], axis=1)


def _stage_segments(stage_ref, x_ref, seg_len):
    pitch = _segment_pitch(seg_len)
    for g, lanes in enumerate(_lane_groups(x_ref.shape[1])):
        for s in range(SUBLANES):
            stage_ref[g, s * pitch:s * pitch + seg_len, :] = x_ref[s * seg_len:(s + 1) * seg_len, lanes]


def _unstage_segments(h_ref, stage_ref, seg_len):
    pitch = _segment_pitch(seg_len)
    for g, lanes in enumerate(_lane_groups(h_ref.shape[1])):
        for s in range(SUBLANES):
            h_ref[s * seg_len:(s + 1) * seg_len, lanes] = stage_ref[g, s * pitch:s * pitch + seg_len, :]


def _shift_segments(v, fill_row, *, down):
    sub = lax.broadcasted_iota(jnp.int32, v.shape, 0)
    if down:
        return jnp.where(sub >= 1, pltpu.roll(v, 1, 0), fill_row)
    return jnp.where(sub < SUBLANES - 1, pltpu.roll(v, SUBLANES - 1, 0), fill_row)


class _SegmentedScan:
    def __init__(self, n, reverse):
        self.n, self.reverse = n, reverse
        self.h, self.a_cum, self.last = [None] * n, [None] * n, None

    def order(self):
        return list(range(self.n - 1, -1, -1) if self.reverse else range(self.n))

    def feed(self, i, a, u):
        if self.last is None:
            self.h[i], self.a_cum[i] = u, a
        else:
            self.h[i], self.a_cum[i] = a * self.h[self.last] + u, a * self.a_cum[self.last]
        self.last = i

    def finish(self, carry):
        h_end, a_end = self.h[self.last], self.a_cum[self.last]
        entering = [None] * SUBLANES
        for s in (range(SUBLANES - 1, -1, -1) if self.reverse else range(SUBLANES)):
            entering[s] = carry
            carry = h_end[s:s + 1, :] + a_end[s:s + 1, :] * carry
        entering = jnp.concatenate(entering, axis=0)
        return [self.h[i] + self.a_cum[i] * entering for i in range(self.n)], carry


def _rglru_conv_gates(x_ref, prev_ref, next_ref, cw_ref, cb_ref, wa_ref, wx_ref, stage_ref, first_tile, last_tile):
    tm = x_ref.shape[0]
    n = tm // SUBLANES
    _stage_segments(stage_ref, x_ref, n)
    xs = [_segment_rows(stage_ref, i, n) for i in range(n)]
    prev = jnp.where(first_tile, 0.0, prev_ref[...])
    nxt = jnp.where(last_tile, 0.0, next_ref[...])
    edge = {-2: _shift_segments(xs[n - 2], prev[HALO - 2:HALO - 1, :], down=True),
            -1: _shift_segments(xs[n - 1], prev[HALO - 1:HALO, :], down=True),
            n: _shift_segments(xs[0], nxt[0:1, :], down=False)}
    tap = lambda i: xs[i] if 0 <= i < n else edge[i]
    cw = [cw_ref[w:w + 1, :] for w in range(RG_CONV_W)]
    cb = cb_ref[...]
    xc = []
    for i in range(n):
        acc = cb + tap(i - RG_CONV_LEFT) * cw[0]
        for w in range(1, RG_CONV_W):
            acc = acc + tap(i - RG_CONV_LEFT + w) * cw[w]
        xc.append(acc)
    xc = jnp.concatenate(xc, axis=0)
    xc16 = xc.astype(BF16)
    return xc, _dot(xc16, wa_ref[...]), _dot(xc16, wx_ref[...])


def _rglru_recurrence(xc, pre_r, pre_i, ba_ref, bx_ref, lam_ref, h_ref, stage_ref, carry_ref, *, reverse):
    n = xc.shape[0] // SUBLANES
    r = jax.nn.sigmoid(pre_r + ba_ref[...])
    gate_i = jax.nn.sigmoid(pre_i + bx_ref[...])
    log_a = (-RG_C) * r * _softplus(-lam_ref[...])
    a = jnp.exp(log_a)
    y = -jnp.tanh(log_a) * (a * a + 1.0)
    u = jnp.where(y > 0.0, y * lax.rsqrt(y), 0.0) * (gate_i * xc)
    scan = _SegmentedScan(n, reverse)
    for i in scan.order():
        vec = slice(i * SUBLANES, (i + 1) * SUBLANES)
        scan.feed(i, a[vec, :], u[vec, :])
    h, carry = scan.finish(carry_ref[...])
    carry_ref[...] = carry
    for i in range(n):
        for g, lanes in enumerate(_lane_groups(D_RG)):
            stage_ref[g, pl.ds(i, SUBLANES, stride=_segment_pitch(n)), :] = h[i][:, lanes]
    _unstage_segments(h_ref, stage_ref, n)


def _rglru_direction(x_ref, prev_ref, next_ref, cw_ref, cb_ref, wa_ref, ba_ref, wx_ref, bx_ref,
                     lam_ref, h_ref, stage_ref, carry_ref, first_tile, last_tile, *, reverse):
    pre = _rglru_conv_gates(x_ref, prev_ref, next_ref, cw_ref, cb_ref, wa_ref, wx_ref, stage_ref,
                            first_tile, last_tile)
    _rglru_recurrence(*pre, ba_ref, bx_ref, lam_ref, h_ref, stage_ref, carry_ref, reverse=reverse)


def _rglru_fwd_kernel(x_ref, prev_ref, next_ref, cw_ref, cb_ref, wa_ref, ba_ref, wx_ref, bx_ref, lam_ref,
                      h_ref, stage_ref, carry_ref):
    j = pl.program_id(1)

    @pl.when(j == 0)
    def _():
        carry_ref[...] = jnp.zeros_like(carry_ref)

    _rglru_direction(x_ref, prev_ref, next_ref, cw_ref, cb_ref, wa_ref, ba_ref, wx_ref, bx_ref, lam_ref,
                     h_ref, stage_ref, carry_ref, j == 0, j == pl.num_programs(1) - 1, reverse=False)


def _rx_tile_specs(tm, t, tile_of):
    halo_per_tile = tm // HALO
    n_halo = t // HALO
    main = pl.BlockSpec((None, tm, D_RG), lambda b, j: (b, tile_of(j), COL_RX))
    prev = pl.BlockSpec(
        (None, HALO, D_RG), lambda b, j: (b, jnp.maximum(tile_of(j) * halo_per_tile - 1, 0), COL_RX))
    nxt = pl.BlockSpec(
        (None, HALO, D_RG), lambda b, j: (b, jnp.minimum((tile_of(j) + 1) * halo_per_tile, n_halo - 1), COL_RX))
    return [main, prev, nxt]


def _rglru_scratch(tm):
    return [pltpu.VMEM((D_RG // LANES, SUBLANES * _segment_pitch(tm // SUBLANES), LANES), F32),
            pltpu.VMEM((1, D_RG), F32)]


def _whole(a):
    return pl.BlockSpec(a.shape, lambda *_: (0,) * a.ndim)


def _rglru_fwd_scan(z3d, conv_w, conv_b, gates):
    bsz, t, _ = z3d.shape
    tm = RGLRU_FWD_TILE
    params = [conv_w, conv_b, *gates]
    return pl.pallas_call(
        _rglru_fwd_kernel,
        grid=(bsz, t // tm),
        in_specs=_rx_tile_specs(tm, t, lambda j: j) + [_whole(p) for p in params],
        out_specs=pl.BlockSpec((None, tm, D_RG), lambda b, j: (b, j, 0)),
        out_shape=jax.ShapeDtypeStruct((bsz, t, D_RG), F32),
        scratch_shapes=_rglru_scratch(tm),
        compiler_params=pltpu.CompilerParams(
            dimension_semantics=("arbitrary", "arbitrary"), vmem_limit_bytes=VMEM_LIMIT),
        name="rglru_fwd_scan",
    )(z3d, z3d, z3d, *params)


def _gelu_tanh(x):
    return 0.5 * x * (1.0 + jnp.tanh(0.7978845608028654 * (x + 0.044715 * (x * x * x))))


def _mix_residual(x, gate, of_ref, ob_ref, g_ref, hf_ref, hb_ref, rgate_ref, hgain_ref, w_ref):
    o = of_ref[...] + ob_ref[...]
    g = _silu(g_ref[...])
    o_rg = ((hf_ref[...] + hb_ref[...]) * _gelu_tanh(rgate_ref[...])).astype(BF16)
    y = _dot(o_rg, w_ref[D_HG:, :])
    for h in range(HG_HEADS):
        sl = _head_lanes(h)
        o_hg = (_rms(o[:, sl], hgain_ref[...]) * g[:, sl]).astype(BF16)
        y = y + _dot(o_hg, w_ref[sl, :])
    return x + gate * y


def _mixout_kernel(x_ref, mod_ref, of_ref, ob_ref, g_ref, hf_ref, rgate_ref, rx_ref, rxprev_ref, rxnext_ref,
                   cw_ref, cb_ref, wa_ref, ba_ref, wx_ref, bx_ref, lam_ref, hgain_ref, wout_ref,
                   gain_ref, w1_ref, w3_ref, w2_ref, fgain_ref, o_ref, hb_ref, stage_ref, carry_ref):
    s = pl.program_id(1)
    n_t = pl.num_programs(1) - 1
    scan_tile = n_t - 1 - s

    def scan():
        _rglru_direction(rx_ref, rxprev_ref, rxnext_ref, cw_ref, cb_ref, wa_ref, ba_ref, wx_ref, bx_ref, lam_ref,
                         hb_ref, stage_ref, carry_ref, scan_tile == 0, scan_tile == n_t - 1, reverse=True)

    def mix_ffn():
        x2 = _mix_residual(x_ref[...], mod_ref[5:6, :], of_ref, ob_ref, g_ref, hf_ref, hb_ref,
                           rgate_ref, hgain_ref, wout_ref)
        x3 = _swiglu_residual(x2, mod_ref, 6, gain_ref, w1_ref, w3_ref, w2_ref)
        o_ref[...] = _rms(x3, fgain_ref[...])

    @pl.when(s == 0)
    def _():
        carry_ref[...] = jnp.zeros_like(carry_ref)
        scan()

    @pl.when(jnp.logical_and(s > 0, s < n_t))
    def _():
        mix_ffn()
        scan()

    @pl.when(s == n_t)
    def _():
        mix_ffn()


def _mixout(x3d, mod, o_f, o_b, z3d, h_f, conv_w, conv_b, gates_bwd, hg_gain, w_out, gain, w1, w3, w2, final_gain):
    bsz, t, d = x3d.shape
    tm = ROW_TILE
    n_t = t // tm
    ffn_tile = lambda s: jnp.minimum(n_t - s, n_t - 1)
    scan_tile = lambda s: jnp.maximum(n_t - 1 - s, 0)

    def rows(width, col=0):
        return pl.BlockSpec((None, tm, width), lambda b, s: (b, ffn_tile(s), col))

    scan_params = [conv_w, conv_b, *gates_bwd]
    ffn_params = [hg_gain, w_out, gain, w1, w3, w2, final_gain]
    return pl.pallas_call(
        _mixout_kernel,
        grid=(bsz, n_t + 1),
        in_specs=[rows(d), pl.BlockSpec((None, N_MOD, d), lambda b, s: (b, 0, 0)),
                  rows(D_HG), rows(D_HG), rows(D_HG, COL_G), rows(D_RG), rows(D_RG, COL_RGATE),
                  *_rx_tile_specs(tm, t, scan_tile),
                  *[_whole(p) for p in scan_params], *[_resident(p.shape) for p in ffn_params]],
        out_specs=rows(d),
        out_shape=jax.ShapeDtypeStruct((bsz, t, d), F32),
        scratch_shapes=[pltpu.VMEM((tm, D_RG), F32), *_rglru_scratch(tm)],
        compiler_params=pltpu.CompilerParams(
            dimension_semantics=("arbitrary", "arbitrary"), vmem_limit_bytes=VMEM_LIMIT),
        name="mixout",
    )(x3d, mod, o_f, o_b, z3d, h_f, z3d, z3d, z3d, z3d, *scan_params, *ffn_params)


def _block_diag(w):
    nb, d, e = w.shape
    eye = jnp.eye(nb, dtype=w.dtype)
    return (eye[:, None, :, None] * w[:, :, None, :]).reshape(nb * d, nb * e)


def kernel(x, c, w_ada, b_ada, ffn1_norm, ffn1_w1, ffn1_w3, ffn1_w2, mix_norm, w_in, hg_lb_fwd, hg_lb_bwd, hg_out_norm, rg_conv_w, rg_conv_b, rg_wa_fwd, rg_ba_fwd, rg_wx_fwd, rg_bx_fwd, rg_lam_fwd, rg_wa_bwd, rg_ba_bwd, rg_wx_bwd, rg_bx_bwd, rg_lam_bwd, w_out, ffn2_norm, ffn2_w1, ffn2_w3, ffn2_w2, final_norm):
    bsz, t, d = x.shape
    depth = w_ada.shape[0]
    assert depth == 1 and hg_lb_fwd.shape[0] == 2, "kernels are written for a single layer"
    assert t % ROW_TILE == 0 and t % RGLRU_FWD_TILE == 0 and t % HGRN_TILE == 0 and HGRN_TILE % HG_CHUNK == 0
    tiles_per_batch = t // ROW_TILE
    row = lambda v: v.reshape(1, -1)
    l = 0

    mod = _modulation(c, w_ada[l], b_ada[l])
    bf16 = lambda w: w[l].astype(BF16)
    flat = lambda a: a.reshape(bsz * t, a.shape[-1])

    x1, z2d = _ffn_inproj(flat(x), mod, row(ffn1_norm[l]), bf16(ffn1_w1), bf16(ffn1_w3), bf16(ffn1_w2),
                          row(mix_norm[l]), bf16(w_in), tiles_per_batch=tiles_per_batch)
    z3d = z2d.reshape(bsz, t, IN_COLS)
    o_f, o_b = _hgrn_scan(z3d, hg_lb_fwd, hg_lb_bwd)

    def gates(wa, ba, wx, bx, lam):
        return (_block_diag(wa[l]).astype(BF16), row(ba[l]), _block_diag(wx[l]).astype(BF16), row(bx[l]),
                row(lam[l]))

    conv_w, conv_b = rg_conv_w[l], row(rg_conv_b[l])
    h_f = _rglru_fwd_scan(z3d, conv_w, conv_b, gates(rg_wa_fwd, rg_ba_fwd, rg_wx_fwd, rg_bx_fwd, rg_lam_fwd))

    return _mixout(x1.reshape(bsz, t, d), mod, o_f, o_b, z3d, h_f, conv_w, conv_b,
                   gates(rg_wa_bwd, rg_ba_bwd, rg_wx_bwd, rg_bx_bwd, rg_lam_bwd), row(hg_out_norm[l]), bf16(w_out),
                   row(ffn2_norm[l]), bf16(ffn2_w1), bf16(ffn2_w3), bf16(ffn2_w2), row(final_norm))
```
